```python
import math
import jax
import jax.numpy as jnp
from jax import lax
import numpy as np

D_MODEL = 2048
BATCH = 8
SEQ = 4096
DEPTH = 4

N_MIXERS = 4
GROUP_WIDTH = D_MODEL // N_MIXERS
HEAD_DIM = 128
N_HEADS = GROUP_WIDTH // HEAD_DIM
MIX_WIDTH = N_MIXERS * GROUP_WIDTH
Q_LORA = 512
KV_LORA = 512
QK_NOPE = 128
QK_ROPE = 64
V_HEAD = HEAD_DIM
MLA_QK_DIM = QK_NOPE + QK_ROPE
DILATED_PAIRS = ((128, 1), (512, 4), (2048, 16))
FORGET_BIAS_INIT = 2.0
BLOCK = 128
ROPE_THETA = 10000.0
FFN_HIDDEN = -(-8 * D_MODEL // (3 * 256)) * 256
EPS = 1e-6
NEG_INF = -1e30
IN_SPLITS = ((Q_LORA, KV_LORA, QK_ROPE)
             + (GROUP_WIDTH,) * 3
             + (GROUP_WIDTH,) * 3 + (N_HEADS,)
             + (GROUP_WIDTH,) * 3)
IN_WIDTH = sum(IN_SPLITS)

kernel_name = "hybrid_parallel_heads_mla_dilated_fox_stickbreak"


def rms_norm(x, gain):
    xf = x.astype(jnp.float32)
    y = xf * lax.rsqrt(jnp.mean(jnp.square(xf), axis=-1, keepdims=True) + EPS)
    return (y * gain.astype(jnp.float32)).astype(x.dtype)


def rope_tables(seq, dim):
    pos = jnp.arange(seq, dtype=jnp.float32)
    inv_freq = ROPE_THETA ** (-jnp.arange(0, dim, 2, dtype=jnp.float32) / dim)
    ang = pos[:, None] * inv_freq[None, :]
    return jnp.cos(ang), jnp.sin(ang)


def apply_rope(x, cos, sin):
    xf = x.astype(jnp.float32)
    x1, x2 = jnp.split(xf, 2, axis=-1)
    c, s = cos[:, None, :], sin[:, None, :]
    return jnp.concatenate([x1 * c - x2 * s, x1 * s + x2 * c], axis=-1).astype(x.dtype)


def split_heads(t):
    b, s, _ = t.shape
    return t.reshape(b, s, N_HEADS, -1)


def to_query_blocks(t):
    b, h, s = t.shape[:3]
    t = t.reshape((b, h, s // BLOCK, BLOCK) + t.shape[3:])
    return jnp.moveaxis(t, 2, 0)


def from_query_blocks(o):
    nb, b, h, _, d = o.shape
    return jnp.moveaxis(o, 0, 2).reshape(b, h, nb * BLOCK, d).transpose(0, 2, 1, 3)


def causal_softmax_attention(q, k, v, scale, cum_log_forget=None):
    s_len = q.shape[1]
    qh, kh, vh = (t.transpose(0, 2, 1, 3) for t in (q, k, v))
    key_pos = jnp.arange(s_len)
    block_idx = jnp.arange(s_len // BLOCK)
    if cum_log_forget is None:
        xs = (to_query_blocks(qh), block_idx)
    else:
        xs = (to_query_blocks(qh), block_idx, to_query_blocks(cum_log_forget))

    def body(args):
        qi, i = args[0], args[1]
        s = jnp.einsum('bhqd,bhkd->bhqk', qi, kh).astype(jnp.float32) * scale
        if cum_log_forget is not None:
            s = s + args[2][..., None] - cum_log_forget[:, :, None, :]
        q_pos = i * BLOCK + jnp.arange(BLOCK)
        s = jnp.where(key_pos[None, :] <= q_pos[:, None], s, NEG_INF)
        p = jax.nn.softmax(s, axis=-1).astype(vh.dtype)
        return jnp.einsum('bhqk,bhkd->bhqd', p, vh)

    return from_query_blocks(lax.map(body, xs))


def mla_attention(q_lat, kv_lat, k_rope, q_norm, w_uq, kv_norm, w_ukv, cos, sin):
    b, s, _ = q_lat.shape
    q = (rms_norm(q_lat, q_norm) @ w_uq).reshape(b, s, N_HEADS, MLA_QK_DIM)
    q = jnp.concatenate([q[..., :QK_NOPE], apply_rope(q[..., QK_NOPE:], cos, sin)], axis=-1)
    kv = (rms_norm(kv_lat, kv_norm) @ w_ukv).reshape(b, s, N_HEADS, QK_NOPE + V_HEAD)
    k_nope, v = kv[..., :QK_NOPE], kv[..., QK_NOPE:]
    k_pe = apply_rope(k_rope[:, :, None, :], cos, sin)
    k = jnp.concatenate([k_nope, jnp.broadcast_to(k_pe, (b, s, N_HEADS, QK_ROPE))], axis=-1)
    return causal_softmax_attention(q, k, v, MLA_QK_DIM ** -0.5)


def banded_window_attention(q, k, v, steps, scale):
    n, h, l, d = q.shape
    pad = (-l) % BLOCK
    cfg = ((0, 0), (0, 0), (0, pad), (0, 0))
    q, k, v = jnp.pad(q, cfg), jnp.pad(k, cfg), jnp.pad(v, cfg)
    nb = (l + pad) // BLOCK
    qb = q.reshape(n, h, nb, BLOCK, d)

    def with_prev(t):
        tb = t.reshape(n, h, nb, BLOCK, d)
        prev = jnp.concatenate([jnp.zeros_like(tb[:, :, :1]), tb[:, :, :-1]], axis=2)
        return jnp.concatenate([prev, tb], axis=3)

    kb, vb = with_prev(k), with_prev(v)
    s = jnp.einsum('nhbqd,nhbkd->nhbqk', qb, kb).astype(jnp.float32) * scale
    q_idx = jnp.arange(BLOCK)
    k_idx = jnp.arange(2 * BLOCK)
    dist = BLOCK + q_idx[:, None] - k_idx[None, :]
    key_pos = (jnp.arange(nb)[:, None] - 1) * BLOCK + k_idx[None, :]
    valid = ((dist >= 0) & (dist <= steps))[None] & (key_pos >= 0)[:, None, :]
    s = jnp.where(valid, s, NEG_INF)
    lse = jax.nn.logsumexp(s, axis=-1)
    p = jnp.exp(s - lse[..., None]).astype(v.dtype)
    out = jnp.einsum('nhbqk,nhbkd->nhbqd', p, vb).reshape(n, h, nb * BLOCK, d)[:, :, :l]
    return out, lse.reshape(n, h, nb * BLOCK)[:, :, :l]


def dilated_window_attention(q, k, v):
    b, s, h, d = q.shape
    scale = d ** -0.5
    outs, lses = [], []
    for window, dilation in DILATED_PAIRS:
        l = s // dilation

        def by_residue(t):
            return t.reshape(b, l, dilation, h, d).transpose(0, 2, 3, 1, 4).reshape(b * dilation, h, l, d)

        o, lse = banded_window_attention(by_residue(q), by_residue(k), by_residue(v),
                                         window // dilation, scale)
        outs.append(o.reshape(b, dilation, h, l, d).transpose(0, 3, 1, 2, 4).reshape(b, s, h, d))
        lses.append(lse.reshape(b, dilation, h, l).transpose(0, 3, 1, 2).reshape(b, s, h))
    weights = jax.nn.softmax(jnp.stack(lses), axis=0).astype(q.dtype)
    return jnp.einsum('gbsh,gbshd->bshd', weights, jnp.stack(outs))


def forgetting_attention(q, k, v, f_logit, f_bias):
    log_f = jax.nn.log_sigmoid((f_logit + f_bias).astype(jnp.float32))
    cum = jnp.cumsum(log_f, axis=1).transpose(0, 2, 1)
    return causal_softmax_attention(q, k, v, HEAD_DIM ** -0.5, cum)


def stick_breaking_attention(q, k, v):
    s_len, d = q.shape[1], q.shape[-1]
    scale = d ** -0.5
    qh, kh, vh = (t.transpose(0, 2, 1, 3) for t in (q, k, v))
    key_pos = jnp.arange(s_len)

    def body(args):
        qi, i = args
        z = jnp.einsum('bhqd,bhkd->bhqk', qi, kh).astype(jnp.float32) * scale
        q_pos = i * BLOCK + jnp.arange(BLOCK)
        past = key_pos[None, :] < q_pos[:, None]
        log_keep = jnp.where(past, jax.nn.log_sigmoid(-z), 0.0)
        between = lax.cumsum(log_keep, axis=3, reverse=True) - log_keep
        a = jnp.where(past, jnp.exp(jax.nn.log_sigmoid(z) + between), 0.0)
        return jnp.einsum('bhqk,bhkd->bhqd', a.astype(vh.dtype), vh)

    xs = (to_query_blocks(qh), jnp.arange(s_len // BLOCK))
    return from_query_blocks(lax.map(body, xs))


def hybrid_mixer(h, w_in, mla_q_norm, w_uq, mla_kv_norm, w_ukv, fox_forget_bias,
                 group_norm, w_out, rope_full, rope_mla):
    b, s, _ = h.shape
    proj = h @ w_in
    offsets = np.cumsum(IN_SPLITS)[:-1].tolist()
    (q_lat, kv_lat, k_rope, q_b, k_b, v_b, q_c, k_c, v_c, f_c, q_d, k_d, v_d) = \
        jnp.split(proj, offsets, axis=-1)
    cos, sin = rope_full
    out_a = mla_attention(q_lat, kv_lat, k_rope, mla_q_norm, w_uq, mla_kv_norm, w_ukv, *rope_mla)
    out_b = dilated_window_attention(apply_rope(split_heads(q_b), cos, sin),
                                     apply_rope(split_heads(k_b), cos, sin), split_heads(v_b))
    out_c = forgetting_attention(split_heads(q_c), split_heads(k_c), split_heads(v_c),
                                 f_c, fox_forget_bias)
    out_d = stick_breaking_attention(split_heads(q_d), split_heads(k_d), split_heads(v_d))
    groups = jnp.stack([o.reshape(b, s, GROUP_WIDTH) for o in (out_a, out_b, out_c, out_d)],
                       axis=2)
    groups = rms_norm(groups, group_norm.reshape(N_MIXERS, GROUP_WIDTH))
    return groups.reshape(b, s, MIX_WIDTH) @ w_out


def _fwd_setup_inputs(seed: int = 0) -> dict:
    key = jax.random.key(seed)
    ks = jax.random.split(key, 16)

    def w(k, shape, fan_in):
        return jax.random.normal(k, shape, jnp.float32) * fan_in ** -0.5

    def gain(k, shape):
        return 1.0 + 0.05 * jax.random.normal(k, shape, jnp.float32)

    return {
        "x": jax.random.normal(ks[0], (BATCH, SEQ, D_MODEL), jnp.float32),
        "attn_norm": gain(ks[1], (DEPTH, D_MODEL)),
        "w_in": w(ks[2], (DEPTH, D_MODEL, IN_WIDTH), D_MODEL),
        "mla_q_norm": gain(ks[3], (DEPTH, Q_LORA)),
        "w_uq": w(ks[4], (DEPTH, Q_LORA, N_HEADS * MLA_QK_DIM), Q_LORA),
        "mla_kv_norm": gain(ks[5], (DEPTH, KV_LORA)),
        "w_ukv": w(ks[6], (DEPTH, KV_LORA, N_HEADS * (QK_NOPE + V_HEAD)), KV_LORA),
        "fox_forget_bias": FORGET_BIAS_INIT + 0.1 * jax.random.normal(ks[7], (DEPTH, N_HEADS), jnp.float32),
        "group_norm": gain(ks[8], (DEPTH, MIX_WIDTH)),
        "w_out": w(ks[9], (DEPTH, MIX_WIDTH, D_MODEL), MIX_WIDTH),
        "ffn_norm": gain(ks[10], (DEPTH, D_MODEL)),
        "w_gate": w(ks[11], (DEPTH, D_MODEL, FFN_HIDDEN), D_MODEL),
        "w_up": w(ks[12], (DEPTH, D_MODEL, FFN_HIDDEN), D_MODEL),
        "w_down": w(ks[13], (DEPTH, FFN_HIDDEN, D_MODEL), FFN_HIDDEN),
        "final_norm": gain(ks[14], (D_MODEL,)),
    }


def _fwd_reference(x, attn_norm, w_in, mla_q_norm, w_uq, mla_kv_norm, w_ukv, fox_forget_bias,
              group_norm, w_out, ffn_norm, w_gate, w_up, w_down, final_norm):
    s_len = x.shape[1]
    rope_full = rope_tables(s_len, HEAD_DIM)
    rope_mla = rope_tables(s_len, QK_ROPE)
    for l in range(DEPTH):
        x = x + hybrid_mixer(rms_norm(x, attn_norm[l]), w_in[l], mla_q_norm[l], w_uq[l],
                             mla_kv_norm[l], w_ukv[l], fox_forget_bias[l], group_norm[l],
                             w_out[l], rope_full, rope_mla)
        h = rms_norm(x, ffn_norm[l])
        x = x + (jax.nn.silu(h @ w_gate[l]) * (h @ w_up[l])) @ w_down[l]
    return rms_norm(x, final_norm)


import jax as _jax
import jax.numpy as _jnp

TWIN_FORMAT = 'train_step'
FWD_PARAMS = ['x', 'attn_norm', 'w_in', 'mla_q_norm', 'w_uq', 'mla_kv_norm', 'w_ukv', 'fox_forget_bias', 'group_norm', 'w_out', 'ffn_norm', 'w_gate', 'w_up', 'w_down', 'final_norm']
TWIN_WEIGHTS = ['attn_norm', 'w_in', 'mla_q_norm', 'w_uq', 'mla_kv_norm', 'w_ukv', 'fox_forget_bias', 'group_norm', 'w_out', 'ffn_norm', 'w_gate', 'w_up', 'w_down', 'final_norm']
TWIN_DIFF_INPUT = 'x'
TWIN_INPUTS = ['x', 'attn_norm', 'w_in', 'mla_q_norm', 'w_uq', 'mla_kv_norm', 'w_ukv', 'fox_forget_bias', 'group_norm', 'w_out', 'ffn_norm', 'w_gate', 'w_up', 'w_down', 'final_norm', 'loss_target', 'm_attn_norm', 'm_w_in', 'm_mla_q_norm', 'm_w_uq', 'm_mla_kv_norm', 'm_w_ukv', 'm_fox_forget_bias', 'm_group_norm', 'm_w_out', 'm_ffn_norm', 'm_w_gate', 'm_w_up', 'm_w_down', 'm_final_norm', 'v_attn_norm', 'v_w_in', 'v_mla_q_norm', 'v_w_uq', 'v_mla_kv_norm', 'v_w_ukv', 'v_fox_forget_bias', 'v_group_norm', 'v_w_out', 'v_ffn_norm', 'v_w_gate', 'v_w_up', 'v_w_down', 'v_final_norm']
TWIN_OUTPUTS = ['loss', 'grad_x', 'grad_attn_norm', 'grad_w_in', 'grad_mla_q_norm', 'grad_w_uq', 'grad_mla_kv_norm', 'grad_w_ukv', 'grad_fox_forget_bias', 'grad_group_norm', 'grad_w_out', 'grad_ffn_norm', 'grad_w_gate', 'grad_w_up', 'grad_w_down', 'grad_final_norm', 'delta_attn_norm', 'delta_w_in', 'delta_mla_q_norm', 'delta_w_uq', 'delta_mla_kv_norm', 'delta_w_ukv', 'delta_fox_forget_bias', 'delta_group_norm', 'delta_w_out', 'delta_ffn_norm', 'delta_w_gate', 'delta_w_up', 'delta_w_down', 'delta_final_norm', 'new_m_attn_norm', 'new_m_w_in', 'new_m_mla_q_norm', 'new_m_w_uq', 'new_m_mla_kv_norm', 'new_m_w_ukv', 'new_m_fox_forget_bias', 'new_m_group_norm', 'new_m_w_out', 'new_m_ffn_norm', 'new_m_w_gate', 'new_m_w_up', 'new_m_w_down', 'new_m_final_norm', 'new_v_attn_norm', 'new_v_w_in', 'new_v_mla_q_norm', 'new_v_w_uq', 'new_v_mla_kv_norm', 'new_v_w_ukv', 'new_v_fox_forget_bias', 'new_v_group_norm', 'new_v_w_out', 'new_v_ffn_norm', 'new_v_w_gate', 'new_v_w_up', 'new_v_w_down', 'new_v_final_norm']
TWIN_LEAF_KINDS = {'loss': 'loss', 'grad_x': 'grad_x', 'grad_attn_norm': 'grad_w', 'grad_w_in': 'grad_w', 'grad_mla_q_norm': 'grad_w', 'grad_w_uq': 'grad_w', 'grad_mla_kv_norm': 'grad_w', 'grad_w_ukv': 'grad_w', 'grad_fox_forget_bias': 'grad_w', 'grad_group_norm': 'grad_w', 'grad_w_out': 'grad_w', 'grad_ffn_norm': 'grad_w', 'grad_w_gate': 'grad_w', 'grad_w_up': 'grad_w', 'grad_w_down': 'grad_w', 'grad_final_norm': 'grad_w', 'delta_attn_norm': 'delta_w', 'delta_w_in': 'delta_w', 'delta_mla_q_norm': 'delta_w', 'delta_w_uq': 'delta_w', 'delta_mla_kv_norm': 'delta_w', 'delta_w_ukv': 'delta_w', 'delta_fox_forget_bias': 'delta_w', 'delta_group_norm': 'delta_w', 'delta_w_out': 'delta_w', 'delta_ffn_norm': 'delta_w', 'delta_w_gate': 'delta_w', 'delta_w_up': 'delta_w', 'delta_w_down': 'delta_w', 'delta_final_norm': 'delta_w', 'new_m_attn_norm': 'new_m', 'new_m_w_in': 'new_m', 'new_m_mla_q_norm': 'new_m', 'new_m_w_uq': 'new_m', 'new_m_mla_kv_norm': 'new_m', 'new_m_w_ukv': 'new_m', 'new_m_fox_forget_bias': 'new_m', 'new_m_group_norm': 'new_m', 'new_m_w_out': 'new_m', 'new_m_ffn_norm': 'new_m', 'new_m_w_gate': 'new_m', 'new_m_w_up': 'new_m', 'new_m_w_down': 'new_m', 'new_m_final_norm': 'new_m', 'new_v_attn_norm': 'new_v', 'new_v_w_in': 'new_v', 'new_v_mla_q_norm': 'new_v', 'new_v_w_uq': 'new_v', 'new_v_mla_kv_norm': 'new_v', 'new_v_w_ukv': 'new_v', 'new_v_fox_forget_bias': 'new_v', 'new_v_group_norm': 'new_v', 'new_v_w_out': 'new_v', 'new_v_ffn_norm': 'new_v', 'new_v_w_gate': 'new_v', 'new_v_w_up': 'new_v', 'new_v_w_down': 'new_v', 'new_v_final_norm': 'new_v'}


def _forward(args):
    return _fwd_reference(*[args[k] for k in FWD_PARAMS])


def _output_shape():
    out = _jax.eval_shape(lambda: _forward(_fwd_setup_inputs(0)))
    return out.shape, out.dtype

N_MICROBATCH = 1
ADAM_LR = 0.001
ADAM_B1 = 0.9
ADAM_B2 = 0.999
ADAM_EPS = 1e-08
ADAM_WD = 0.01
ADAM_STEP = 10
PER_EXAMPLE_BATCH_AXIS = {'x': 0, 'loss_target': 0}
SHARED_INPUTS = []
_WEIGHT_DTYPES = {'attn_norm': _jnp.float32, 'w_in': _jnp.float32, 'mla_q_norm': _jnp.float32, 'w_uq': _jnp.float32, 'mla_kv_norm': _jnp.float32, 'w_ukv': _jnp.float32, 'fox_forget_bias': _jnp.float32, 'group_norm': _jnp.float32, 'w_out': _jnp.float32, 'ffn_norm': _jnp.float32, 'w_gate': _jnp.float32, 'w_up': _jnp.float32, 'w_down': _jnp.float32, 'final_norm': _jnp.float32}
MOMENT_SCALE = {'attn_norm': 1.188910e-01, 'w_in': 7.232073e-02, 'mla_q_norm': 4.295294e-02, 'w_uq': 2.994680e-02, 'mla_kv_norm': 1.785435e-01, 'w_ukv': 1.273177e-01, 'fox_forget_bias': 2.964069e-01, 'group_norm': 1.149787e-01, 'w_out': 1.186552e-01, 'ffn_norm': 4.322756e-02, 'w_gate': 1.811349e-02, 'w_up': 1.916485e-02, 'w_down': 3.167648e-02, 'final_norm': 1.616302e+01}


def _to_microbatches(a, axis):
    t = _jnp.moveaxis(a, axis, 0)
    t = t.reshape((N_MICROBATCH, t.shape[0] // N_MICROBATCH) + t.shape[1:])
    return _jnp.moveaxis(t, 1, axis + 1)


def setup_inputs(seed: int = 0) -> dict:
    inp = _fwd_setup_inputs(seed)
    key = _jax.random.fold_in(_jax.random.key(seed), 7919)
    shape, _ = _output_shape()
    out = dict(inp)
    out["loss_target"] = _jax.random.normal(_jax.random.fold_in(key, 0), shape, _jnp.float32)
    for i, name in enumerate(TWIN_WEIGHTS):
        w = inp[name].astype(_jnp.float32)
        if MOMENT_SCALE is None:
            s = _jnp.sqrt(_jnp.mean(_jnp.square(w)) + 1e-30)
        else:
            s = MOMENT_SCALE[name]
        km, kv = _jax.random.split(_jax.random.fold_in(key, i + 1))
        out[name] = w
        out["m_" + name] = s * _jax.random.normal(km, w.shape, _jnp.float32)
        out["v_" + name] = (s * s) * _jax.random.uniform(kv, w.shape, _jnp.float32, 0.5, 1.5)
    if N_MICROBATCH > 1:
        for name, axis in PER_EXAMPLE_BATCH_AXIS.items():
            out[name] = _to_microbatches(out[name], axis)
    return {'x': out['x'], 'attn_norm': out['attn_norm'], 'w_in': out['w_in'], 'mla_q_norm': out['mla_q_norm'], 'w_uq': out['w_uq'], 'mla_kv_norm': out['mla_kv_norm'], 'w_ukv': out['w_ukv'], 'fox_forget_bias': out['fox_forget_bias'], 'group_norm': out['group_norm'], 'w_out': out['w_out'], 'ffn_norm': out['ffn_norm'], 'w_gate': out['w_gate'], 'w_up': out['w_up'], 'w_down': out['w_down'], 'final_norm': out['final_norm'], 'loss_target': out['loss_target'], 'm_attn_norm': out['m_attn_norm'], 'm_w_in': out['m_w_in'], 'm_mla_q_norm': out['m_mla_q_norm'], 'm_w_uq': out['m_w_uq'], 'm_mla_kv_norm': out['m_mla_kv_norm'], 'm_w_ukv': out['m_w_ukv'], 'm_fox_forget_bias': out['m_fox_forget_bias'], 'm_group_norm': out['m_group_norm'], 'm_w_out': out['m_w_out'], 'm_ffn_norm': out['m_ffn_norm'], 'm_w_gate': out['m_w_gate'], 'm_w_up': out['m_w_up'], 'm_w_down': out['m_w_down'], 'm_final_norm': out['m_final_norm'], 'v_attn_norm': out['v_attn_norm'], 'v_w_in': out['v_w_in'], 'v_mla_q_norm': out['v_mla_q_norm'], 'v_w_uq': out['v_w_uq'], 'v_mla_kv_norm': out['v_mla_kv_norm'], 'v_w_ukv': out['v_w_ukv'], 'v_fox_forget_bias': out['v_fox_forget_bias'], 'v_group_norm': out['v_group_norm'], 'v_w_out': out['v_w_out'], 'v_ffn_norm': out['v_ffn_norm'], 'v_w_gate': out['v_w_gate'], 'v_w_up': out['v_w_up'], 'v_w_down': out['v_w_down'], 'v_final_norm': out['v_final_norm']}


def _loss(weights, diff, rest, loss_target):
    with _jax.named_scope("forward"):
        args = {**rest, TWIN_DIFF_INPUT: diff, **{k: w.astype(_WEIGHT_DTYPES[k]) for k, w in weights.items()}}
        y = _forward(args)
    with _jax.named_scope("loss_head"):
        err = _jnp.square(y.astype(_jnp.float32) - loss_target)
        return 0.5 * _jnp.sum(_jnp.mean(err, axis=-1)) if err.ndim else 0.5 * err


def _adamw(w, g, m, v):
    m = ADAM_B1 * m + (1.0 - ADAM_B1) * g
    v = ADAM_B2 * v + (1.0 - ADAM_B2) * _jnp.square(g)
    m_hat = m / (1.0 - ADAM_B1 ** ADAM_STEP)
    v_hat = v / (1.0 - ADAM_B2 ** ADAM_STEP)
    delta = -ADAM_LR * (m_hat / (_jnp.sqrt(v_hat) + ADAM_EPS) + ADAM_WD * w)
    return delta, m, v


def reference(x, attn_norm, w_in, mla_q_norm, w_uq, mla_kv_norm, w_ukv, fox_forget_bias, group_norm, w_out, ffn_norm, w_gate, w_up, w_down, final_norm, loss_target, m_attn_norm, m_w_in, m_mla_q_norm, m_w_uq, m_mla_kv_norm, m_w_ukv, m_fox_forget_bias, m_group_norm, m_w_out, m_ffn_norm, m_w_gate, m_w_up, m_w_down, m_final_norm, v_attn_norm, v_w_in, v_mla_q_norm, v_w_uq, v_mla_kv_norm, v_w_ukv, v_fox_forget_bias, v_group_norm, v_w_out, v_ffn_norm, v_w_gate, v_w_up, v_w_down, v_final_norm):
    given = dict(x=x, attn_norm=attn_norm, w_in=w_in, mla_q_norm=mla_q_norm, w_uq=w_uq, mla_kv_norm=mla_kv_norm, w_ukv=w_ukv, fox_forget_bias=fox_forget_bias, group_norm=group_norm, w_out=w_out, ffn_norm=ffn_norm, w_gate=w_gate, w_up=w_up, w_down=w_down, final_norm=final_norm, loss_target=loss_target, m_attn_norm=m_attn_norm, m_w_in=m_w_in, m_mla_q_norm=m_mla_q_norm, m_w_uq=m_w_uq, m_mla_kv_norm=m_mla_kv_norm, m_w_ukv=m_w_ukv, m_fox_forget_bias=m_fox_forget_bias, m_group_norm=m_group_norm, m_w_out=m_w_out, m_ffn_norm=m_ffn_norm, m_w_gate=m_w_gate, m_w_up=m_w_up, m_w_down=m_w_down, m_final_norm=m_final_norm, v_attn_norm=v_attn_norm, v_w_in=v_w_in, v_mla_q_norm=v_mla_q_norm, v_w_uq=v_w_uq, v_mla_kv_norm=v_mla_kv_norm, v_w_ukv=v_w_ukv, v_fox_forget_bias=v_fox_forget_bias, v_group_norm=v_group_norm, v_w_out=v_w_out, v_ffn_norm=v_ffn_norm, v_w_gate=v_w_gate, v_w_up=v_w_up, v_w_down=v_w_down, v_final_norm=v_final_norm)
    weights = {n: given[n] for n in TWIN_WEIGHTS}
    shared = {n: given[n] for n in SHARED_INPUTS}
    per_example = {n: given[n] for n in ['x']}
    grad_fn = _jax.value_and_grad(_loss, argnums=(0, 1))

    def one_microbatch(ex, loss_target):
        ex = dict(ex)
        diff = ex.pop(TWIN_DIFF_INPUT)
        return grad_fn(weights, diff, {**shared, **ex}, loss_target)

    if N_MICROBATCH == 1:
        loss, (grad_w, grad_x) = one_microbatch(per_example, given["loss_target"])
    else:
        def body(carry, xs):
            loss_sum, grad_sum = carry
            l_k, (gw_k, gx_k) = one_microbatch(xs[0], xs[1])
            with _jax.named_scope("update"):
                return (loss_sum + l_k, _jax.tree.map(_jnp.add, grad_sum, gw_k)), gx_k

        init = (_jnp.zeros((), _jnp.float32), _jax.tree.map(_jnp.zeros_like, weights))
        (loss, grad_w), grad_x = _jax.lax.scan(body, init, (per_example, given["loss_target"]))
    with _jax.named_scope("update"):
        delta_w, new_m, new_v = {}, {}, {}
        for n in TWIN_WEIGHTS:
            delta_w[n], new_m[n], new_v[n] = _adamw(weights[n], grad_w[n], given["m_" + n], given["v_" + n])
    return (loss, grad_x, *[grad_w[n] for n in TWIN_WEIGHTS], *[delta_w[n] for n in TWIN_WEIGHTS],
            *[new_m[n] for n in TWIN_WEIGHTS], *[new_v[n] for n in TWIN_WEIGHTS])
```

```python
import functools
import math

import jax
import jax.numpy as jnp
from jax import lax
from jax.experimental import pallas as pl
from jax.experimental.pallas import tpu as pltpu

F32 = jnp.float32
BF16 = jnp.bfloat16
MESH = pl.DeviceIdType.MESH

HEAD_DIM = 128
Q_LORA = 512
KV_LORA = 512
QK_NOPE = 128
QK_ROPE = 64
DILATED_PAIRS = ((128, 1), (512, 4), (2048, 16))
MAX_WINDOW = max(w for w, _ in DILATED_PAIRS)
ROPE_THETA = 10000.0
EPS = 1e-6
NEG_INF = -1e30
ADAM_LR = 0.001
ADAM_B1 = 0.9
ADAM_B2 = 0.999
ADAM_EPS = 1e-08
ADAM_WD = 0.01
ADAM_STEP = 10

LANES = 128
VMEM_LIMIT_BYTES = 56 * 1024 * 1024
ROW_ALIGN = 1024


def _params(semantics=None):
    return pltpu.CompilerParams(dimension_semantics=semantics, vmem_limit_bytes=VMEM_LIMIT_BYTES)


MATMUL_VMEM_BUDGET = 40 * 1024 * 1024


def _fit_tile(n, cap, estimate):
    t = _tile(n, cap, LANES)
    while estimate(t) > MATMUL_VMEM_BUDGET and t > LANES:
        t = _tile(n, t - LANES, LANES)
    return t


def _tile(n, cap, align):
    if n <= cap:
        return n
    t = cap - cap % align
    while t >= align:
        if n % t == 0:
            return t
        t -= align
    raise ValueError(f"no tile for {n} under {cap}")


def _mm_nn(a, b, res, out_dtype, name):
    m, k = a.shape
    _, n = b.shape
    tm = _tile(m, 1024, 128)
    res_bytes = 0 if res is None else res.dtype.itemsize
    tn = _fit_tile(n, 1024, lambda t: 2 * (tm * k * a.dtype.itemsize + k * t * b.dtype.itemsize
                                            + tm * t * (jnp.dtype(out_dtype).itemsize + res_bytes)))

    def body(*refs):
        a_ref, b_ref = refs[0], refs[1]
        o_ref = refs[-1]
        acc = jnp.dot(a_ref[...].astype(BF16), b_ref[...].astype(BF16), preferred_element_type=F32)
        if res is not None:
            acc = acc + refs[2][...].astype(F32)
        o_ref[...] = acc.astype(o_ref.dtype)

    in_specs = [pl.BlockSpec((tm, k), lambda i, j: (i, 0)), pl.BlockSpec((k, tn), lambda i, j: (0, j))]
    args = [a, b]
    if res is not None:
        in_specs.append(pl.BlockSpec((tm, tn), lambda i, j: (i, j)))
        args.append(res)
    return pl.pallas_call(
        body, name=name, grid=(m // tm, n // tn), in_specs=in_specs,
        out_specs=pl.BlockSpec((tm, tn), lambda i, j: (i, j)),
        out_shape=jax.ShapeDtypeStruct((m, n), out_dtype),
        compiler_params=_params(("parallel", "arbitrary")))(*args)


def _mm_nt(a, b, out_dtype, name):
    m, n = a.shape
    k, _ = b.shape
    tm = _tile(m, 1024, 128)
    tk = _fit_tile(k, 1024, lambda t: 2 * (tm * n * a.dtype.itemsize + t * n * b.dtype.itemsize
                                            + tm * t * jnp.dtype(out_dtype).itemsize))

    def body(a_ref, b_ref, o_ref):
        acc = lax.dot_general(a_ref[...].astype(BF16), b_ref[...].astype(BF16),
                              (((1,), (1,)), ((), ())), preferred_element_type=F32)
        o_ref[...] = acc.astype(o_ref.dtype)

    return pl.pallas_call(
        body, name=name, grid=(m // tm, k // tk),
        in_specs=[pl.BlockSpec((tm, n), lambda i, j: (i, 0)), pl.BlockSpec((tk, n), lambda i, j: (j, 0))],
        out_specs=pl.BlockSpec((tm, tk), lambda i, j: (i, j)),
        out_shape=jax.ShapeDtypeStruct((m, k), out_dtype),
        compiler_params=_params(("parallel", "arbitrary")))(a, b)


def _mm_tn(a, b, out_dtype, name):
    m, k = a.shape
    _, n = b.shape
    tk = _tile(k, 512, LANES)
    tn = _fit_tile(n, 1024, lambda t: 2 * (m * tk * a.dtype.itemsize + m * t * b.dtype.itemsize
                                            + tk * t * jnp.dtype(out_dtype).itemsize))

    def body(a_ref, b_ref, o_ref):
        acc = lax.dot_general(a_ref[...].astype(BF16), b_ref[...].astype(BF16),
                              (((0,), (0,)), ((), ())), preferred_element_type=F32)
        o_ref[...] = acc.astype(o_ref.dtype)

    return pl.pallas_call(
        body, name=name, grid=(k // tk, n // tn),
        in_specs=[pl.BlockSpec((m, tk), lambda i, j: (0, i)), pl.BlockSpec((m, tn), lambda i, j: (0, j))],
        out_specs=pl.BlockSpec((tk, tn), lambda i, j: (i, j)),
        out_shape=jax.ShapeDtypeStruct((k, n), out_dtype),
        compiler_params=_params(("parallel", "arbitrary")))(a, b)


def _matmul_op(out_dtype, with_res, tag):
    def fwd_only(a, w, res):
        return _mm_nn(a, w, res if with_res else None, out_dtype, f"mm_{tag}")

    @jax.custom_vjp
    def op(a, w, res):
        return fwd_only(a, w, res)

    def fwd(a, w, res):
        return fwd_only(a, w, res), (a, w, res)

    def bwd(saved, dc):
        a, w, res = saved
        da = _mm_nt(dc, w, a.dtype, f"mm_{tag}_da")
        dw = _mm_tn(a, dc, w.dtype, f"mm_{tag}_dw")
        dres = dc.astype(res.dtype) if with_res else jnp.zeros_like(res)
        return da, dw, dres

    op.defvjp(fwd, bwd)
    return op


def matmul(a, w, out_dtype, tag):
    return _matmul_op(out_dtype, False, tag)(a, w, jnp.zeros((), F32))


def matmul_res(a, w, res, tag):
    return _matmul_op(res.dtype, True, tag)(a, w, res)


def _rms_fwd_call(x, gain, groups, out_dtype, name):
    s, w = x.shape
    gw = w // groups
    ts = _tile(s, 512, 128)

    def body(x_ref, g_ref, o_ref):
        for g in range(groups):
            cols = slice(g * gw, (g + 1) * gw)
            xv = x_ref[:, cols].astype(F32)
            r = lax.rsqrt(jnp.mean(xv * xv, axis=1, keepdims=True) + EPS)
            o_ref[:, cols] = (xv * r * g_ref[:, cols]).astype(o_ref.dtype)

    return pl.pallas_call(
        body, name=name, grid=(s // ts,),
        in_specs=[pl.BlockSpec((ts, w), lambda i: (i, 0)), pl.BlockSpec((1, w), lambda i: (0, 0))],
        out_specs=pl.BlockSpec((ts, w), lambda i: (i, 0)),
        out_shape=jax.ShapeDtypeStruct((s, w), out_dtype),
        compiler_params=_params(("parallel",)))(x, gain.reshape(1, w))


def _rms_bwd_call(x, gain, dy, groups, name):
    s, w = x.shape
    gw = w // groups
    ts = _tile(s, 512, 128)

    def body(x_ref, g_ref, dy_ref, dx_ref, dg_ref):
        @pl.when(pl.program_id(0) == 0)
        def _():
            dg_ref[...] = jnp.zeros_like(dg_ref)

        for g in range(groups):
            cols = slice(g * gw, (g + 1) * gw)
            xv = x_ref[:, cols].astype(F32)
            dyv = dy_ref[:, cols].astype(F32)
            r = lax.rsqrt(jnp.mean(xv * xv, axis=1, keepdims=True) + EPS)
            xhat = xv * r
            gdy = dyv * g_ref[:, cols]
            dx = r * (gdy - xhat * jnp.mean(gdy * xhat, axis=1, keepdims=True))
            dx_ref[:, cols] = dx.astype(dx_ref.dtype)
            dg_ref[:, cols] += jnp.sum(dyv * xhat, axis=0, keepdims=True)

    return pl.pallas_call(
        body, name=name, grid=(s // ts,),
        in_specs=[pl.BlockSpec((ts, w), lambda i: (i, 0)), pl.BlockSpec((1, w), lambda i: (0, 0)),
                  pl.BlockSpec((ts, w), lambda i: (i, 0))],
        out_specs=[pl.BlockSpec((ts, w), lambda i: (i, 0)), pl.BlockSpec((1, w), lambda i: (0, 0))],
        out_shape=[jax.ShapeDtypeStruct((s, w), x.dtype), jax.ShapeDtypeStruct((1, w), F32)],
        compiler_params=_params(("arbitrary",)))(x, gain.reshape(1, w), dy)


def rms_norm(x, gain, groups, out_dtype, tag):
    @jax.custom_vjp
    def op(x, gain):
        return _rms_fwd_call(x, gain, groups, out_dtype, f"rms_{tag}")

    def fwd(x, gain):
        return _rms_fwd_call(x, gain, groups, out_dtype, f"rms_{tag}"), (x, gain)

    def bwd(saved, dy):
        x, gain = saved
        dx, dg = _rms_bwd_call(x, gain, dy, groups, f"rms_{tag}_bwd")
        return dx, dg.reshape(gain.shape)

    op.defvjp(fwd, bwd)
    return op(x, gain)


def _rope_call(x, cos_t, sin_t, transpose, name):
    s, w = x.shape
    ts = _tile(s, 512, 128)
    half = LANES // 2

    def body(x_ref, c_ref, s_ref, o_ref):
        c, sn = c_ref[...], s_ref[...]
        for j in range(w // LANES):
            cols = slice(j * LANES, (j + 1) * LANES)
            xv = x_ref[:, cols].astype(F32)
            if transpose:
                y = xv * c + pltpu.roll(xv * sn, half, 1)
            else:
                y = xv * c + pltpu.roll(xv, half, 1) * sn
            o_ref[:, cols] = y.astype(o_ref.dtype)

    return pl.pallas_call(
        body, name=name, grid=(s // ts,),
        in_specs=[pl.BlockSpec((ts, w), lambda i: (i, 0)), pl.BlockSpec((ts, LANES), lambda i: (i, 0)),
                  pl.BlockSpec((ts, LANES), lambda i: (i, 0))],
        out_specs=pl.BlockSpec((ts, w), lambda i: (i, 0)),
        out_shape=jax.ShapeDtypeStruct((s, w), x.dtype),
        compiler_params=_params(("parallel",)))(x, cos_t, sin_t)


def rope(x, cos_t, sin_t, tag):
    @jax.custom_vjp
    def op(x):
        return _rope_call(x, cos_t, sin_t, False, f"rope_{tag}")

    def fwd(x):
        return _rope_call(x, cos_t, sin_t, False, f"rope_{tag}"), None

    def bwd(_, dy):
        return (_rope_call(dy, cos_t, sin_t, True, f"rope_{tag}_bwd"),)

    op.defvjp(fwd, bwd)
    return op(x)


def rope_tables(seq, dim):
    half = dim // 2
    pos = jnp.arange(seq, dtype=F32)
    inv_freq = ROPE_THETA ** (-jnp.arange(0, dim, 2, dtype=F32) / dim)
    ang = pos[:, None] * inv_freq[None, :]
    cos, sin = jnp.cos(ang), jnp.sin(ang)
    pad = jnp.zeros((seq, LANES // 2 - half), F32)
    return (jnp.concatenate([cos, pad, cos, pad], axis=1),
            jnp.concatenate([-sin, pad, sin, pad], axis=1))


def _attn_blocks(s):
    b = min(256, s)
    return b, b


def _score_mask(delta, mode):
    if mode != "dilated":
        return None, delta >= 0
    mult = jnp.zeros(delta.shape, F32)
    for window, dilation in DILATED_PAIRS:
        hit = delta <= window
        if dilation > 1:
            hit = hit & ((delta & (dilation - 1)) == 0)
        mult = mult + jnp.where(hit, 1.0, 0.0)
    bias = jnp.where(mult > 2.5, math.log(3.0), jnp.where(mult > 1.5, math.log(2.0), 0.0))
    return bias, (delta >= 0) & (mult > 0.5)


def _kv_range(qi, bq, bk, mode):
    hi = (qi * bq + bq - 1) // bk
    lo = jnp.maximum(qi * bq - MAX_WINDOW, 0) // bk if mode == "dilated" else 0
    return lo, hi


def _attn_fwd_call(q, k, v, cq, ck, heads, scale, mode, name):
    s = q.shape[0]
    dk, dv = q.shape[1] // heads, v.shape[1] // heads
    bq, bk = _attn_blocks(s)
    nq, nk = s // bq, s // bk
    fox = mode == "fox"

    def body(*refs):
        q_ref, k_ref, v_ref = refs[:3]
        o_ref, o32_ref, lse_ref = refs[-3:]
        qi = pl.program_id(1)
        qv = q_ref[...].astype(BF16)
        q_pos = qi * bq + lax.broadcasted_iota(jnp.int32, (bq, bk), 0)
        k_col = lax.broadcasted_iota(jnp.int32, (bq, bk), 1)
        lo, hi = _kv_range(qi, bq, bk, mode)

        def step(kj, carry):
            m, l, acc = carry
            rows = pl.ds(pl.multiple_of(kj * bk, bk), bk)
            kb = k_ref[rows, :].astype(BF16)
            vb = v_ref[rows, :].astype(BF16)
            sc = lax.dot_general(qv, kb, (((1,), (1,)), ((), ())), preferred_element_type=F32) * scale
            if fox:
                sc = sc + refs[3][...] - refs[4][kj]
            bias, mask = _score_mask(q_pos - (kj * bk + k_col), mode)
            if bias is not None:
                sc = sc + bias
            sc = jnp.where(mask, sc, NEG_INF)
            m_new = jnp.maximum(m, jnp.max(sc, axis=1, keepdims=True))
            alpha = jnp.exp(m - m_new)
            p = jnp.exp(sc - m_new)
            l = l * alpha + jnp.sum(p, axis=1, keepdims=True)
            acc = acc * alpha + jnp.dot(p.astype(BF16), vb, preferred_element_type=F32)
            return m_new, l, acc

        init = (jnp.full((bq, 1), NEG_INF, F32), jnp.zeros((bq, 1), F32), jnp.zeros((bq, dv), F32))
        m, l, acc = lax.fori_loop(lo, hi + 1, step, init)
        out = acc / l
        o_ref[...] = out.astype(o_ref.dtype)
        o32_ref[...] = out
        lse_ref[...] = m + jnp.log(l)

    in_specs = [pl.BlockSpec((bq, dk), lambda h, i: (i, h)), pl.BlockSpec((s, dk), lambda h, i: (0, h)),
                pl.BlockSpec((s, dv), lambda h, i: (0, h))]
    args = [q, k, v]
    if fox:
        in_specs += [pl.BlockSpec((None, bq, 1), lambda h, i: (h, i, 0)),
                     pl.BlockSpec((None, nk, 1, bk), lambda h, i: (h, 0, 0, 0))]
        args += [cq, ck]
    return pl.pallas_call(
        body, name=name, grid=(heads, nq), in_specs=in_specs,
        out_specs=[pl.BlockSpec((bq, dv), lambda h, i: (i, h)), pl.BlockSpec((bq, dv), lambda h, i: (i, h)),
                   pl.BlockSpec((None, bq, 1), lambda h, i: (h, i, 0))],
        out_shape=[jax.ShapeDtypeStruct((s, heads * dv), BF16), jax.ShapeDtypeStruct((s, heads * dv), F32),
                   jax.ShapeDtypeStruct((heads, s, 1), F32)],
        compiler_params=_params(("parallel", "arbitrary")))(*args)


def _attn_bwd_call(q, k, v, cq, ck, o, lse, do, heads, scale, mode, name):
    s = q.shape[0]
    dk, dv = q.shape[1] // heads, v.shape[1] // heads
    bq, bk = _attn_blocks(s)
    nq, nk = s // bq, s // bk
    fox = mode == "fox"
    n_in = 8 if fox else 6

    def body(*refs):
        q_ref, k_ref, v_ref = refs[:3]
        o_ref, lse_ref, do_ref = refs[n_in - 3:n_in]
        outs = refs[n_in:]
        dq_ref, dk_ref, dv_ref = outs[:3]
        scratch = outs[5:] if fox else outs[3:]
        dk_acc, dv_acc = scratch[:2]
        qi = pl.program_id(1)

        @pl.when(qi == 0)
        def _():
            dk_acc[...] = jnp.zeros_like(dk_acc)
            dv_acc[...] = jnp.zeros_like(dv_acc)
            if fox:
                scratch[2][...] = jnp.zeros_like(scratch[2])

        qv = q_ref[...].astype(BF16)
        dov = do_ref[...].astype(BF16)
        delta_o = jnp.sum(dov.astype(F32) * o_ref[...], axis=1, keepdims=True)
        lse = lse_ref[...]
        q_pos = qi * bq + lax.broadcasted_iota(jnp.int32, (bq, bk), 0)
        k_col = lax.broadcasted_iota(jnp.int32, (bq, bk), 1)
        lo, hi = _kv_range(qi, bq, bk, mode)

        def step(kj, carry):
            dq, dcq = carry
            rows = pl.ds(pl.multiple_of(kj * bk, bk), bk)
            kb = k_ref[rows, :].astype(BF16)
            vb = v_ref[rows, :].astype(BF16)
            sc = lax.dot_general(qv, kb, (((1,), (1,)), ((), ())), preferred_element_type=F32) * scale
            if fox:
                sc = sc + refs[3][...] - refs[4][kj]
            bias, mask = _score_mask(q_pos - (kj * bk + k_col), mode)
            if bias is not None:
                sc = sc + bias
            sc = jnp.where(mask, sc, NEG_INF)
            p = jnp.exp(sc - lse)
            dv_acc[rows, :] += lax.dot_general(p.astype(BF16), dov, (((0,), (0,)), ((), ())),
                                               preferred_element_type=F32)
            dp = lax.dot_general(dov, vb, (((1,), (1,)), ((), ())), preferred_element_type=F32)
            ds = p * (dp - delta_o)
            dsb = ds.astype(BF16)
            dk_acc[rows, :] += lax.dot_general(dsb, qv, (((0,), (0,)), ((), ())), preferred_element_type=F32)
            if fox:
                scratch[2][kj] -= jnp.sum(ds, axis=0, keepdims=True)
                dcq = dcq + jnp.sum(ds, axis=1, keepdims=True)
            return dq + jnp.dot(dsb, kb, preferred_element_type=F32), dcq

        dq, dcq = lax.fori_loop(lo, hi + 1, step, (jnp.zeros((bq, dk), F32), jnp.zeros((bq, 1), F32)))
        dq_ref[...] = (dq * scale).astype(dq_ref.dtype)
        if fox:
            outs[4][...] = dcq

        @pl.when(qi == nq - 1)
        def _():
            dk_ref[...] = (dk_acc[...] * scale).astype(dk_ref.dtype)
            dv_ref[...] = dv_acc[...].astype(dv_ref.dtype)
            if fox:
                outs[3][...] = scratch[2][...]

    in_specs = [pl.BlockSpec((bq, dk), lambda h, i: (i, h)), pl.BlockSpec((s, dk), lambda h, i: (0, h)),
                pl.BlockSpec((s, dv), lambda h, i: (0, h))]
    args = [q, k, v]
    if fox:
        in_specs += [pl.BlockSpec((None, bq, 1), lambda h, i: (h, i, 0)),
                     pl.BlockSpec((None, nk, 1, bk), lambda h, i: (h, 0, 0, 0))]
        args += [cq, ck]
    in_specs += [pl.BlockSpec((bq, dv), lambda h, i: (i, h)), pl.BlockSpec((None, bq, 1), lambda h, i: (h, i, 0)),
                 pl.BlockSpec((bq, dv), lambda h, i: (i, h))]
    args += [o, lse, do]
    out_specs = [pl.BlockSpec((bq, dk), lambda h, i: (i, h)), pl.BlockSpec((s, dk), lambda h, i: (0, h)),
                 pl.BlockSpec((s, dv), lambda h, i: (0, h))]
    out_shape = [jax.ShapeDtypeStruct(q.shape, q.dtype), jax.ShapeDtypeStruct(k.shape, k.dtype),
                 jax.ShapeDtypeStruct(v.shape, v.dtype)]
    scratch_shapes = [pltpu.VMEM((s, dk), F32), pltpu.VMEM((s, dv), F32)]
    if fox:
        out_specs.append(pl.BlockSpec((None, nk, 1, bk), lambda h, i: (h, 0, 0, 0)))
        out_shape.append(jax.ShapeDtypeStruct((heads, nk, 1, bk), F32))
        out_specs.append(pl.BlockSpec((None, bq, 1), lambda h, i: (h, i, 0)))
        out_shape.append(jax.ShapeDtypeStruct((heads, s, 1), F32))
        scratch_shapes.append(pltpu.VMEM((nk, 1, bk), F32))
    return pl.pallas_call(
        body, name=name, grid=(heads, nq), in_specs=in_specs, out_specs=out_specs, out_shape=out_shape,
        scratch_shapes=scratch_shapes, compiler_params=_params(("arbitrary", "arbitrary")))(*args)


def attention(q, k, v, cum, heads, scale, mode, tag):
    s = q.shape[0]
    _, bk = _attn_blocks(s)

    def layouts(cum):
        return cum.reshape(heads, s, 1), cum.reshape(heads, s // bk, 1, bk)

    @jax.custom_vjp
    def op(q, k, v, cum):
        cq, ck = layouts(cum)
        return _attn_fwd_call(q, k, v, cq, ck, heads, scale, mode, f"attn_{tag}")[0]

    def fwd(q, k, v, cum):
        cq, ck = layouts(cum)
        o, o32, lse = _attn_fwd_call(q, k, v, cq, ck, heads, scale, mode, f"attn_{tag}")
        return o, (q, k, v, cum, o32, lse)

    def bwd(saved, do):
        q, k, v, cum, o, lse = saved
        cq, ck = layouts(cum)
        res = _attn_bwd_call(q, k, v, cq, ck, o, lse, do, heads, scale, mode, f"attn_{tag}_bwd")
        dcum = res[3].reshape(heads, s) + res[4].reshape(heads, s) if mode == "fox" else jnp.zeros_like(cum)
        return res[0], res[1], res[2], dcum

    op.defvjp(fwd, bwd)
    return op(q, k, v, cum)


def _log_sigmoid(z):
    return jnp.minimum(z, 0.0) - jnp.log(1.0 + jnp.exp(-jnp.abs(z)))


def _tri_dot(x, tri):
    hi = x.astype(BF16)
    lo = (x - hi.astype(F32)).astype(BF16)
    return jnp.dot(hi, tri, preferred_element_type=F32) + jnp.dot(lo, tri, preferred_element_type=F32)


def _sb_call(q, k, v, keep_all, do, heads, scale, backward, name):
    s = q.shape[0]
    d = q.shape[1] // heads
    bq, bk = _attn_blocks(s)
    nq = s // bq

    def body(*refs):
        q_ref, k_ref, v_ref = refs[:3]
        qi = pl.program_id(1)
        qv = q_ref[...].astype(BF16)
        q_pos = qi * bq + lax.broadcasted_iota(jnp.int32, (bq, bk), 0)
        k_col = lax.broadcasted_iota(jnp.int32, (bq, bk), 1)
        r_i = lax.broadcasted_iota(jnp.int32, (bk, bk), 0)
        c_i = lax.broadcasted_iota(jnp.int32, (bk, bk), 1)
        hi = (qi * bq + bq - 1) // bk

        def logits(kj):
            rows = pl.ds(pl.multiple_of(kj * bk, bk), bk)
            kb = k_ref[rows, :].astype(BF16)
            z = lax.dot_general(qv, kb, (((1,), (1,)), ((), ())), preferred_element_type=F32) * scale
            past = (kj * bk + k_col) < q_pos
            ls = _log_sigmoid(z)
            lk = jnp.where(past, ls - z, 0.0)
            return rows, kb, z, past, ls, lk

        if not backward:
            o_ref, keep_ref = refs[3:5]
            after = jnp.where(r_i > c_i, 1.0, 0.0).astype(BF16)

            def step(it, carry):
                keep_right, acc = carry
                rows, _, _, past, ls, lk = logits(hi - it)
                a = jnp.where(past, jnp.exp(ls + _tri_dot(lk, after) + keep_right), 0.0)
                acc = acc + jnp.dot(a.astype(BF16), v_ref[rows, :].astype(BF16), preferred_element_type=F32)
                return keep_right + jnp.sum(lk, axis=1, keepdims=True), acc

            keep, acc = lax.fori_loop(0, hi + 1, step, (jnp.zeros((bq, 1), F32), jnp.zeros((bq, d), F32)))
            o_ref[...] = acc.astype(o_ref.dtype)
            keep_ref[...] = keep
            return

        keep_ref, do_ref, dq_ref, dk_ref, dv_ref, dk_acc, dv_acc = refs[3:]

        @pl.when(qi == 0)
        def _():
            dk_acc[...] = jnp.zeros_like(dk_acc)
            dv_acc[...] = jnp.zeros_like(dv_acc)

        dov = do_ref[...].astype(BF16)
        keep_total = keep_ref[...]
        upto = jnp.where(r_i <= c_i, 1.0, 0.0).astype(BF16)
        before = jnp.where(r_i < c_i, 1.0, 0.0).astype(BF16)

        def step(kj, carry):
            keep_left, g_left, dq = carry
            rows, kb, z, past, ls, lk = logits(kj)
            between = keep_total - (keep_left + _tri_dot(lk, upto))
            a = jnp.where(past, jnp.exp(ls + between), 0.0)
            vb = v_ref[rows, :].astype(BF16)
            da = lax.dot_general(dov, vb, (((1,), (1,)), ((), ())), preferred_element_type=F32)
            g = a * da
            g_before = g_left + _tri_dot(g, before)
            dz = jnp.where(past, g * jnp.exp(ls - z) - jnp.exp(ls) * g_before, 0.0)
            dzb = dz.astype(BF16)
            dk_acc[rows, :] += lax.dot_general(dzb, qv, (((0,), (0,)), ((), ())), preferred_element_type=F32)
            dv_acc[rows, :] += lax.dot_general(a.astype(BF16), dov, (((0,), (0,)), ((), ())),
                                               preferred_element_type=F32)
            dq = dq + jnp.dot(dzb, kb, preferred_element_type=F32)
            return (keep_left + jnp.sum(lk, axis=1, keepdims=True),
                    g_left + jnp.sum(g, axis=1, keepdims=True), dq)

        init = (jnp.zeros((bq, 1), F32), jnp.zeros((bq, 1), F32), jnp.zeros((bq, d), F32))
        _, _, dq = lax.fori_loop(0, hi + 1, step, init)
        dq_ref[...] = (dq * scale).astype(dq_ref.dtype)

        @pl.when(qi == nq - 1)
        def _():
            dk_ref[...] = (dk_acc[...] * scale).astype(dk_ref.dtype)
            dv_ref[...] = dv_acc[...].astype(dv_ref.dtype)

    blk_q = pl.BlockSpec((bq, d), lambda h, i: (i, h))
    blk_kv = pl.BlockSpec((s, d), lambda h, i: (0, h))
    blk_row = pl.BlockSpec((None, bq, 1), lambda h, i: (h, i, 0))
    if not backward:
        return pl.pallas_call(
            body, name=name, grid=(heads, nq), in_specs=[blk_q, blk_kv, blk_kv], out_specs=[blk_q, blk_row],
            out_shape=[jax.ShapeDtypeStruct(q.shape, BF16), jax.ShapeDtypeStruct((heads, s, 1), F32)],
            compiler_params=_params(("parallel", "arbitrary")))(q, k, v)
    return pl.pallas_call(
        body, name=name, grid=(heads, nq), in_specs=[blk_q, blk_kv, blk_kv, blk_row, blk_q],
        out_specs=[blk_q, blk_kv, blk_kv],
        out_shape=[jax.ShapeDtypeStruct(q.shape, q.dtype), jax.ShapeDtypeStruct(k.shape, k.dtype),
                   jax.ShapeDtypeStruct(v.shape, v.dtype)],
        scratch_shapes=[pltpu.VMEM((s, d), F32), pltpu.VMEM((s, d), F32)],
        compiler_params=_params(("arbitrary", "arbitrary")))(q, k, v, keep_all, do)


def stick_breaking(q, k, v, heads, scale, tag):
    @jax.custom_vjp
    def op(q, k, v):
        return _sb_call(q, k, v, None, None, heads, scale, False, f"sb_{tag}")[0]

    def fwd(q, k, v):
        o, keep_all = _sb_call(q, k, v, None, None, heads, scale, False, f"sb_{tag}")
        return o, (q, k, v, keep_all)

    def bwd(saved, do):
        q, k, v, keep_all = saved
        return tuple(_sb_call(q, k, v, keep_all, do, heads, scale, True, f"sb_{tag}_bwd"))

    op.defvjp(fwd, bwd)
    return op(q, k, v)


def _tri_dot3(x, tri):
    hi = x.astype(BF16)
    r1 = x - hi.astype(F32)
    mid = r1.astype(BF16)
    lo = (r1 - mid.astype(F32)).astype(BF16)
    return (jnp.dot(hi, tri, preferred_element_type=F32) + jnp.dot(mid, tri, preferred_element_type=F32)
            + jnp.dot(lo, tri, preferred_element_type=F32))


def _gate_call(f, bias_b, dcum, name):
    heads, r, _ = f.shape
    backward = dcum is not None

    def body(*refs):
        f_ref, b_ref = refs[:2]
        r_i = lax.broadcasted_iota(jnp.int32, (LANES, LANES), 0)
        c_i = lax.broadcasted_iota(jnp.int32, (LANES, LANES), 1)
        x = f_ref[...] + b_ref[...]
        if not backward:
            o_ref, ls_ref = refs[2:]
            ls_ref[...] = _log_sigmoid(x)
            o_ref[...] = _tri_dot3(ls_ref[...], jnp.where(r_i <= c_i, 1.0, 0.0).astype(BF16))

            def row(i, carry):
                o_ref[pl.ds(i, 1), :] = o_ref[pl.ds(i, 1), :] + carry
                return carry + jnp.sum(ls_ref[pl.ds(i, 1), :], axis=1, keepdims=True)

            lax.fori_loop(0, r, row, jnp.zeros((1, 1), F32))
            return

        dc_ref, df_ref, db_ref, acc_ref = refs[2:]
        acc_ref[...] = _tri_dot3(dc_ref[...], jnp.where(r_i >= c_i, 1.0, 0.0).astype(BF16))

        def row(it, carry):
            i = r - 1 - it
            acc_ref[pl.ds(i, 1), :] = acc_ref[pl.ds(i, 1), :] + carry
            return carry + jnp.sum(dc_ref[pl.ds(i, 1), :], axis=1, keepdims=True)

        lax.fori_loop(0, r, row, jnp.zeros((1, 1), F32))
        df = acc_ref[...] * jnp.exp(_log_sigmoid(-x))
        df_ref[...] = df
        lane = lax.broadcasted_iota(jnp.int32, (1, LANES), 1)
        db_ref[...] = jnp.where(lane == 0, jnp.sum(df), 0.0)

    blk = pl.BlockSpec((None, r, LANES), lambda h: (h, 0, 0))
    blk_b = pl.BlockSpec((None, 1, LANES), lambda h: (h, 0, 0))
    if not backward:
        return pl.pallas_call(
            body, name=name, grid=(heads,), in_specs=[blk, blk_b], out_specs=blk,
            out_shape=jax.ShapeDtypeStruct(f.shape, F32), scratch_shapes=[pltpu.VMEM((r, LANES), F32)],
            compiler_params=_params(("parallel",)))(f, bias_b)
    return pl.pallas_call(
        body, name=name, grid=(heads,), in_specs=[blk, blk_b, blk], out_specs=[blk, blk_b],
        out_shape=[jax.ShapeDtypeStruct(f.shape, F32), jax.ShapeDtypeStruct(bias_b.shape, F32)],
        scratch_shapes=[pltpu.VMEM((r, LANES), F32)],
        compiler_params=_params(("parallel",)))(f, bias_b, dcum)


def forget_gate_cumsum(f, bias_b, tag):
    @jax.custom_vjp
    def op(f, bias_b):
        return _gate_call(f, bias_b, None, f"gate_{tag}")

    def fwd(f, bias_b):
        return _gate_call(f, bias_b, None, f"gate_{tag}"), (f, bias_b)

    def bwd(saved, dcum):
        return tuple(_gate_call(saved[0], saved[1], dcum, f"gate_{tag}_bwd"))

    op.defvjp(fwd, bwd)
    return op(f, bias_b)


def _swiglu_call(g, u, dact, name):
    s, n = g.shape
    ts = _tile(s, 256, 128)
    backward = dact is not None

    def body(*refs):
        gv = refs[0][...].astype(F32)
        uv = refs[1][...].astype(F32)
        sig = 1.0 / (1.0 + jnp.exp(-gv))
        if not backward:
            refs[2][...] = (gv * sig * uv).astype(refs[2].dtype)
            return
        dv = refs[2][...].astype(F32)
        refs[3][...] = (dv * uv * sig * (1.0 + gv * (1.0 - sig))).astype(refs[3].dtype)
        refs[4][...] = (dv * gv * sig).astype(refs[4].dtype)

    blk = pl.BlockSpec((ts, n), lambda i: (i, 0))
    shape = jax.ShapeDtypeStruct((s, n), g.dtype)
    if not backward:
        return pl.pallas_call(body, name=name, grid=(s // ts,), in_specs=[blk, blk], out_specs=blk,
                              out_shape=shape, compiler_params=_params(("parallel",)))(g, u)
    return pl.pallas_call(body, name=name, grid=(s // ts,), in_specs=[blk, blk, blk], out_specs=[blk, blk],
                          out_shape=[shape, shape], compiler_params=_params(("parallel",)))(g, u, dact)


def swiglu(g, u, tag):
    @jax.custom_vjp
    def op(g, u):
        return _swiglu_call(g, u, None, f"swiglu_{tag}")

    def fwd(g, u):
        return _swiglu_call(g, u, None, f"swiglu_{tag}"), (g, u)

    def bwd(saved, dact):
        return tuple(_swiglu_call(saved[0], saved[1], dact, f"swiglu_{tag}_bwd"))

    op.defvjp(fwd, bwd)
    return op(g, u)


def final_norm_loss(x, gain, target):
    s, d = x.shape
    ts = _tile(s, 256, 128)

    def body(x_ref, g_ref, t_ref, sq_ref, dx_ref, dg_ref):
        @pl.when(pl.program_id(0) == 0)
        def _():
            sq_ref[...] = jnp.zeros_like(sq_ref)
            dg_ref[...] = jnp.zeros_like(dg_ref)

        xv = x_ref[...]
        r = lax.rsqrt(jnp.mean(xv * xv, axis=1, keepdims=True) + EPS)
        xhat = xv * r
        err = xhat * g_ref[...] - t_ref[...]
        sq_ref[...] += jnp.sum(err * err)
        dy = err * (1.0 / d)
        gdy = dy * g_ref[...]
        dx_ref[...] = r * (gdy - xhat * jnp.mean(gdy * xhat, axis=1, keepdims=True))
        dg_ref[...] += jnp.sum(dy * xhat, axis=0, keepdims=True)

    blk = pl.BlockSpec((ts, d), lambda i: (i, 0))
    row = pl.BlockSpec((1, d), lambda i: (0, 0))
    return pl.pallas_call(
        body, name="final_norm_loss", grid=(s // ts,), in_specs=[blk, row, blk],
        out_specs=[pl.BlockSpec((1, LANES), lambda i: (0, 0)), blk, row],
        out_shape=[jax.ShapeDtypeStruct((1, LANES), F32), jax.ShapeDtypeStruct((s, d), F32),
                   jax.ShapeDtypeStruct((1, d), F32)],
        compiler_params=_params(("arbitrary",)))(x, gain.reshape(1, d), target)


def adamw(w, g, m, v, name):
    rows, cols = w.shape
    tr = _tile(rows, 256, 8)
    c1 = 1.0 - ADAM_B1 ** ADAM_STEP
    c2 = 1.0 - ADAM_B2 ** ADAM_STEP

    def body(w_ref, g_ref, m_ref, v_ref, d_ref, nm_ref, nv_ref):
        gv = g_ref[...]
        m_new = ADAM_B1 * m_ref[...] + (1.0 - ADAM_B1) * gv
        v_new = ADAM_B2 * v_ref[...] + (1.0 - ADAM_B2) * (gv * gv)
        d_ref[...] = -ADAM_LR * ((m_new / c1) / (jnp.sqrt(v_new / c2) + ADAM_EPS) + ADAM_WD * w_ref[...])
        nm_ref[...] = m_new
        nv_ref[...] = v_new

    blk = pl.BlockSpec((tr, cols), lambda i: (i, 0))
    shape = jax.ShapeDtypeStruct((rows, cols), F32)
    return pl.pallas_call(body, name=name, grid=(rows // tr,), in_specs=[blk] * 4, out_specs=[blk] * 3,
                          out_shape=[shape] * 3, compiler_params=_params(("parallel",)))(w, g, m, v)


ANY = pl.BlockSpec(memory_space=pl.ANY)


def _position():
    return lax.axis_index("x"), lax.axis_index("y"), lax.axis_index("c")


def gather_chips(shard, name):
    r, lanes = shard.shape

    def body(x_ref, out_ref, send_sems, recv_sems, local_sem):
        x, y, c = _position()
        chips = [(1 - x, y), (x, 1 - y), (1 - x, 1 - y)]
        mine = pltpu.make_async_copy(x_ref, out_ref.at[2 * x + y], local_sem)
        mine.start()

        def copy(k, slot, chip):
            return pltpu.make_async_remote_copy(
                src_ref=x_ref, dst_ref=out_ref.at[slot], send_sem=send_sems.at[k], recv_sem=recv_sems.at[k],
                device_id=(chip[0], chip[1], c), device_id_type=MESH)

        sends = [copy(k, 2 * x + y, chip) for k, chip in enumerate(chips)]
        for cp in sends:
            cp.start()
        for k, chip in enumerate(chips):
            copy(k, 2 * chip[0] + chip[1], chip).wait_recv()
        for cp in sends:
            cp.wait_send()
        mine.wait()

    return pl.pallas_call(
        body, name=name, in_specs=[ANY], out_specs=ANY,
        out_shape=jax.ShapeDtypeStruct((4, r, lanes), shard.dtype),
        scratch_shapes=[pltpu.SemaphoreType.DMA((3,)), pltpu.SemaphoreType.DMA((3,)), pltpu.SemaphoreType.DMA])(shard)


def exchange_all(parts, name):
    _, _, rh, lanes = parts.shape

    def body(p_ref, out_ref, send_sems, recv_sems, local_sem):
        x, y, c = _position()
        me = 4 * x + 2 * y + c
        mine = pltpu.make_async_copy(p_ref.at[2 * x + y, c], out_ref.at[me], local_sem)
        mine.start()

        def peer(k):
            return (1 - x if k & 4 else x, 1 - y if k & 2 else y, 1 - c if k & 1 else c)

        def copy(k, to_me):
            px, py, pc = peer(k)
            slot = 4 * px + 2 * py + pc if to_me else me
            return pltpu.make_async_remote_copy(
                src_ref=p_ref.at[2 * px + py, pc], dst_ref=out_ref.at[slot],
                send_sem=send_sems.at[k - 1], recv_sem=recv_sems.at[k - 1],
                device_id=(px, py, pc), device_id_type=MESH)

        sends = [copy(k, False) for k in range(1, 8)]
        for cp in sends:
            cp.start()
        for k in range(1, 8):
            copy(k, True).wait_recv()
        for cp in sends:
            cp.wait_send()
        mine.wait()

    return pl.pallas_call(
        body, name=name, in_specs=[ANY], out_specs=ANY,
        out_shape=jax.ShapeDtypeStruct((8, rh, lanes), parts.dtype),
        scratch_shapes=[pltpu.SemaphoreType.DMA((7,)), pltpu.SemaphoreType.DMA((7,)), pltpu.SemaphoreType.DMA])(parts)


def broadcast_all(block, name):
    r, lanes = block.shape

    def body(b_ref, out_ref, send_sems, recv_sems, local_sem):
        x, y, c = _position()
        me = 4 * x + 2 * y + c
        mine = pltpu.make_async_copy(b_ref, out_ref.at[me], local_sem)
        mine.start()

        def copy(k, to_me):
            px, py, pc = (1 - x if k & 4 else x, 1 - y if k & 2 else y, 1 - c if k & 1 else c)
            slot = 4 * px + 2 * py + pc if to_me else me
            return pltpu.make_async_remote_copy(
                src_ref=b_ref, dst_ref=out_ref.at[slot], send_sem=send_sems.at[k - 1],
                recv_sem=recv_sems.at[k - 1], device_id=(px, py, pc), device_id_type=MESH)

        sends = [copy(k, False) for k in range(1, 8)]
        for cp in sends:
            cp.start()
        for k in range(1, 8):
            copy(k, True).wait_recv()
        for cp in sends:
            cp.wait_send()
        mine.wait()

    return pl.pallas_call(
        body, name=name, in_specs=[ANY], out_specs=ANY,
        out_shape=jax.ShapeDtypeStruct((8, r, lanes), block.dtype),
        scratch_shapes=[pltpu.SemaphoreType.DMA((7,)), pltpu.SemaphoreType.DMA((7,)), pltpu.SemaphoreType.DMA])(block)


def swap_halves(half, name):
    rh, lanes = half.shape

    def body(h_ref, out_ref, send_sem, recv_sem, local_sem):
        x, y, c = _position()
        mine = pltpu.make_async_copy(h_ref, out_ref.at[c], local_sem)
        mine.start()
        send = pltpu.make_async_remote_copy(
            src_ref=h_ref, dst_ref=out_ref.at[c], send_sem=send_sem, recv_sem=recv_sem,
            device_id=(x, y, 1 - c), device_id_type=MESH)
        send.start()
        pltpu.make_async_remote_copy(
            src_ref=h_ref, dst_ref=out_ref.at[1 - c], send_sem=send_sem, recv_sem=recv_sem,
            device_id=(x, y, 1 - c), device_id_type=MESH).wait_recv()
        send.wait_send()
        mine.wait()

    return pl.pallas_call(
        body, name=name, in_specs=[ANY], out_specs=ANY,
        out_shape=jax.ShapeDtypeStruct((2, rh, lanes), half.dtype),
        scratch_shapes=[pltpu.SemaphoreType.DMA, pltpu.SemaphoreType.DMA, pltpu.SemaphoreType.DMA])(half)


def sum_slots(stack, name):
    n, r, lanes = stack.shape
    tr = _tile(r, 1024, 16)

    def body(s_ref, o_ref):
        acc = s_ref[0].astype(F32)
        for i in range(1, n):
            acc = acc + s_ref[i].astype(F32)
        o_ref[...] = acc

    return pl.pallas_call(
        body, name=name, grid=(r // tr,), in_specs=[pl.BlockSpec((n, tr, lanes), lambda i: (0, i, 0))],
        out_specs=pl.BlockSpec((tr, lanes), lambda i: (i, 0)), out_shape=jax.ShapeDtypeStruct((r, lanes), F32),
        compiler_params=_params(("parallel",)))(stack)


def _pack(arrays, dtype, row_align):
    flat = jnp.concatenate([a.reshape(-1).astype(dtype) for a in arrays])
    rows = -(-flat.shape[0] // LANES)
    rows = -(-rows // row_align) * row_align
    return jnp.pad(flat, (0, rows * LANES - flat.shape[0])).reshape(rows, LANES)


def _unpack(flat, shapes):
    flat = flat.reshape(flat.shape[:-2] + (-1,))
    out, at = [], 0
    for shp in shapes:
        n = math.prod(shp)
        out.append(flat[..., at:at + n].reshape(flat.shape[:-1] + tuple(shp)))
        at += n
    return out


def _layer_weights(gathered, shard_shapes, d_model, heads):
    gw = heads * HEAD_DIM
    w_in, w_uq, w_ukv, w_out, w_gate, w_up, w_down = _unpack(gathered, shard_shapes)

    def cols(w):
        return jnp.moveaxis(w, 0, 1).reshape(w.shape[1], -1)

    def rows(w):
        return w.reshape(-1, w.shape[2])

    w_in, w_uq, w_ukv, w_gate, w_up = cols(w_in), cols(w_uq), cols(w_ukv), cols(w_gate), cols(w_up)
    w_out, w_down = rows(w_out), rows(w_down)

    def spread(w):
        half = QK_ROPE // 2
        z = jnp.zeros(w.shape[:-1] + (LANES // 2 - half,), w.dtype)
        return jnp.concatenate([w[..., :half], z, w[..., half:], z], axis=-1)

    o_rope = Q_LORA + KV_LORA
    o_b = o_rope + QK_ROPE
    o_f = o_b + 6 * gw
    o_d = o_f + heads
    w_main = jnp.concatenate([w_in[:, :o_rope], spread(w_in[:, o_rope:o_b]), w_in[:, o_b:o_f], w_in[:, o_d:]], axis=1)
    w_fc = jnp.pad(w_in[:, o_f:o_d], ((0, 0), (0, LANES - heads)))
    uq = w_uq.reshape(Q_LORA, heads, QK_NOPE + QK_ROPE)
    w_uq_re = jnp.concatenate([uq[:, :, :QK_NOPE].reshape(Q_LORA, -1), spread(uq[:, :, QK_NOPE:]).reshape(Q_LORA, -1)],
                              axis=1)
    ukv = w_ukv.reshape(KV_LORA, heads, QK_NOPE + HEAD_DIM)
    w_ukv_re = jnp.concatenate([ukv[:, :, :QK_NOPE].reshape(KV_LORA, -1), ukv[:, :, QK_NOPE:].reshape(KV_LORA, -1)],
                               axis=1)
    return w_main, w_fc, w_uq_re, w_ukv_re, w_out, w_gate, w_up, w_down


def _layer(x, mats, smalls, tables, heads, tag):
    s, d_model = x.shape
    gw = heads * HEAD_DIM
    w_main, w_fc, w_uq_re, w_ukv_re, w_out, w_gate, w_up, w_down = mats
    attn_norm, q_norm, kv_norm, f_bias, group_norm, ffn_norm = smalls
    (cos_f, sin_f), (cos_m, sin_m) = tables
    no_cum = jnp.zeros((heads, s), F32)

    h = rms_norm(x, attn_norm, 1, BF16, f"attn_{tag}")
    proj = matmul(h, w_main, BF16, f"in_{tag}")
    f_logit = matmul(h, w_fc, F32, f"fc_{tag}")
    at = [0]

    def take(n):
        at[0] += n
        return proj[:, at[0] - n:at[0]]

    q_lat, kv_lat, k_rope = take(Q_LORA), take(KV_LORA), take(LANES)
    qk_b, v_b = take(2 * gw), take(gw)
    q_c, k_c, v_c = take(gw), take(gw), take(gw)
    q_d, k_d, v_d = take(gw), take(gw), take(gw)

    q_all = matmul(rms_norm(q_lat, q_norm, 1, BF16, f"qlat_{tag}"), w_uq_re, BF16, f"uq_{tag}")
    kv_all = matmul(rms_norm(kv_lat, kv_norm, 1, BF16, f"kvlat_{tag}"), w_ukv_re, BF16, f"ukv_{tag}")
    q_pe = rope(q_all[:, gw:], cos_m, sin_m, f"qpe_{tag}")
    k_pe = rope(k_rope, cos_m, sin_m, f"kpe_{tag}")
    q_a = jnp.stack([q_all[:, :gw].reshape(s, heads, LANES), q_pe.reshape(s, heads, LANES)], axis=2).reshape(s, 2 * gw)
    k_a = jnp.stack([kv_all[:, :gw].reshape(s, heads, LANES),
                     jnp.broadcast_to(k_pe[:, None, :], (s, heads, LANES))], axis=2).reshape(s, 2 * gw)
    out_a = attention(q_a, k_a, kv_all[:, gw:], no_cum, heads, (QK_NOPE + QK_ROPE) ** -0.5, "causal", f"a_{tag}")

    qk_b = rope(qk_b, cos_f, sin_f, f"qkb_{tag}")
    out_b = attention(qk_b[:, :gw], qk_b[:, gw:], v_b, no_cum, heads, HEAD_DIM ** -0.5, "dilated", f"b_{tag}")

    f_rows = f_logit[:, :heads].T.reshape(heads, s // LANES, LANES)
    bias_b = jnp.broadcast_to(f_bias[:, None, None], (heads, 1, LANES))
    cum = forget_gate_cumsum(f_rows, bias_b, tag).reshape(heads, s)
    out_c = attention(q_c, k_c, v_c, cum, heads, HEAD_DIM ** -0.5, "fox", f"c_{tag}")

    out_d = stick_breaking(q_d, k_d, v_d, heads, HEAD_DIM ** -0.5, f"d_{tag}")

    groups = jnp.concatenate([out_a, out_b, out_c, out_d], axis=1)
    x = matmul_res(rms_norm(groups, group_norm, 4, BF16, f"group_{tag}"), w_out, x, f"out_{tag}")

    h2 = rms_norm(x, ffn_norm, 1, BF16, f"ffn_{tag}")
    act = swiglu(matmul(h2, w_gate, BF16, f"gate_{tag}"), matmul(h2, w_up, BF16, f"up_{tag}"), tag)
    return matmul_res(act, w_down, x, f"down_{tag}")


def kernel(x, attn_norm, w_in, mla_q_norm, w_uq, mla_kv_norm, w_ukv, fox_forget_bias, group_norm, w_out, ffn_norm, w_gate, w_up, w_down, final_norm, loss_target, m_attn_norm, m_w_in, m_mla_q_norm, m_w_uq, m_mla_kv_norm, m_w_ukv, m_fox_forget_bias, m_group_norm, m_w_out, m_ffn_norm, m_w_gate, m_w_up, m_w_down, m_final_norm, v_attn_norm, v_w_in, v_mla_q_norm, v_w_uq, v_mla_kv_norm, v_w_ukv, v_fox_forget_bias, v_group_norm, v_w_out, v_ffn_norm, v_w_gate, v_w_up, v_w_down, v_final_norm):
    depth = w_in.shape[0]
    _, s, d_model = x.shape
    heads = d_model // 4 // HEAD_DIM
    big = [w_in, w_uq, w_ukv, w_out, w_gate, w_up, w_down]
    big_m = [m_w_in, m_w_uq, m_w_ukv, m_w_out, m_w_gate, m_w_up, m_w_down]
    big_v = [v_w_in, v_w_uq, v_w_ukv, v_w_out, v_w_gate, v_w_up, v_w_down]
    small = [attn_norm, mla_q_norm, mla_kv_norm, fox_forget_bias, group_norm, ffn_norm]
    small_m = [m_attn_norm, m_mla_q_norm, m_mla_kv_norm, m_fox_forget_bias, m_group_norm, m_ffn_norm, m_final_norm]
    small_v = [v_attn_norm, v_mla_q_norm, v_mla_kv_norm, v_fox_forget_bias, v_group_norm, v_ffn_norm, v_final_norm]
    shard_shapes = [w.shape[1:] for w in big]
    tables = (rope_tables(s, HEAD_DIM), rope_tables(s, QK_ROPE))

    gathered = [gather_chips(_pack([w[l] for w in big], BF16, ROW_ALIGN), f"gather_{l}") for l in range(depth)]

    def trunk(gathered, small, x):
        for l in range(depth):
            mats = _layer_weights(gathered[l], shard_shapes, d_model, heads)
            x = _layer(x, mats, [p[l] for p in small], tables, heads, str(l))
        return x

    x_out, vjp = jax.vjp(trunk, gathered, small, x[0])
    sq, dx_out, d_final = final_norm_loss(x_out, final_norm, loss_target[0])
    loss = 0.5 / d_model * lax.psum(sq[0, 0], ("x", "y", "c"))
    d_gathered, d_small, dx = vjp(dx_out)

    big_grads = []
    for l in range(depth):
        rows = d_gathered[l].shape[1]
        parts = d_gathered[l].reshape(4, 2, rows // 2, LANES)
        half = sum_slots(exchange_all(parts, f"scatter_{l}"), f"sum_{l}")
        full = swap_halves(half, f"swap_{l}").reshape(rows, LANES)
        big_grads.append(_unpack(full, shard_shapes))
    big_g = [jnp.stack([big_grads[l][i] for l in range(depth)]) for i in range(len(big))]

    small_all = small + [final_norm]
    packed = _pack(d_small + [d_final], F32, 8)
    small_sum = sum_slots(broadcast_all(packed, "small_gather"), "small_sum")
    small_g = _unpack(small_sum, [p.shape for p in small_all])

    def update(w, g, m, v, name):
        shp = w.shape
        two_d = (-1, shp[-1])
        return [o.reshape(shp) for o in adamw(w.reshape(two_d), g.reshape(two_d), m.reshape(two_d),
                                              v.reshape(two_d), name)]

    big_u = [update(w, g, m, v, f"adamw_big{i}") for i, (w, g, m, v) in enumerate(zip(big, big_g, big_m, big_v))]
    small_u = adamw(_pack(small_all, F32, 8), small_sum, _pack(small_m, F32, 8), _pack(small_v, F32, 8), "adamw_small")
    small_u = [_unpack(u, [p.shape for p in small_all]) for u in small_u]

    def ordered(bigs, smalls):
        a_n, q_n, kv_n, f_b, g_n, f_n, fin = smalls
        wi, uq, ukv, wo, wg, wu, wd = bigs
        return [a_n, wi, q_n, uq, kv_n, ukv, f_b, g_n, wo, f_n, wg, wu, wd, fin]

    grads = ordered(big_g, small_g)
    deltas = ordered([u[0] for u in big_u], small_u[0])
    new_m = ordered([u[1] for u in big_u], small_u[1])
    new_v = ordered([u[2] for u in big_u], small_u[2])
    return (loss, dx[None], *grads, *deltas, *new_m, *new_v)
```

```python
import functools
import math

import jax
import jax.numpy as jnp
from jax import lax
from jax.experimental import pallas as pl
from jax.experimental.pallas import tpu as pltpu

F32 = jnp.float32
BF16 = jnp.bfloat16
MESH = pl.DeviceIdType.MESH

HEAD_DIM = 128
Q_LORA = 512
KV_LORA = 512
QK_NOPE = 128
QK_ROPE = 64
DILATED_PAIRS = ((128, 1), (512, 4), (2048, 16))
MAX_WINDOW = max(w for w, _ in DILATED_PAIRS)
ROPE_THETA = 10000.0
EPS = 1e-6
NEG_INF = -1e30
ADAM_LR = 0.001
ADAM_B1 = 0.9
ADAM_B2 = 0.999
ADAM_EPS = 1e-08
ADAM_WD = 0.01
ADAM_STEP = 10

LANES = 128
VMEM_LIMIT_BYTES = 56 * 1024 * 1024


def _params(semantics=None):
    return pltpu.CompilerParams(dimension_semantics=semantics, vmem_limit_bytes=VMEM_LIMIT_BYTES)


MATMUL_VMEM_BUDGET = 40 * 1024 * 1024


def _fit_tile(n, cap, estimate):
    t = _tile(n, cap, LANES)
    while estimate(t) > MATMUL_VMEM_BUDGET and t > LANES:
        t = _tile(n, t - LANES, LANES)
    return t


def _tile(n, cap, align):
    if n <= cap:
        return n
    t = cap - cap % align
    while t >= align:
        if n % t == 0:
            return t
        t -= align
    raise ValueError(f"no tile for {n} under {cap}")


def _col_blocks(b):
    return (b.shape[0], b.shape[2]) if b.ndim == 3 else (1, b.shape[1])


def _mm_nn(a, b, res, out_dtype, name):
    m, k = a.shape
    jn, np_ = _col_blocks(b)
    n = jn * np_
    tm = _tile(m, 1024, 128)
    res_bytes = 0 if res is None else res.dtype.itemsize
    tn = _fit_tile(np_, 1024, lambda t: 2 * (tm * k * a.dtype.itemsize + k * t * b.dtype.itemsize
                                              + tm * t * (jnp.dtype(out_dtype).itemsize + res_bytes)))
    nb = np_ // tn

    def body(*refs):
        a_ref, b_ref = refs[0], refs[1]
        o_ref = refs[-1]
        acc = jnp.dot(a_ref[...].astype(BF16), b_ref[...].astype(BF16), preferred_element_type=F32)
        if res is not None:
            acc = acc + refs[2][...].astype(F32)
        o_ref[...] = acc.astype(o_ref.dtype)

    if b.ndim == 3:
        b_spec = pl.BlockSpec((None, k, tn), lambda i, j: (j // nb, 0, j % nb))
    else:
        b_spec = pl.BlockSpec((k, tn), lambda i, j: (0, j))
    in_specs = [pl.BlockSpec((tm, k), lambda i, j: (i, 0)), b_spec]
    args = [a, b]
    if res is not None:
        in_specs.append(pl.BlockSpec((tm, tn), lambda i, j: (i, j)))
        args.append(res)
    return pl.pallas_call(
        body, name=name, grid=(m // tm, n // tn), in_specs=in_specs,
        out_specs=pl.BlockSpec((tm, tn), lambda i, j: (i, j)),
        out_shape=jax.ShapeDtypeStruct((m, n), out_dtype),
        compiler_params=_params(("parallel", "arbitrary")))(*args)


def _mm_nt(a, b, out_dtype, name):
    m, n = a.shape
    jn, np_ = _col_blocks(b)
    k = b.shape[-2]
    tm = _tile(m, 1024, 128)
    tk = _fit_tile(k, 1024, lambda t: 2 * (tm * n * a.dtype.itemsize + t * n * b.dtype.itemsize
                                            + tm * t * jnp.dtype(out_dtype).itemsize))

    def body(a_ref, b_ref, o_ref):
        acc = None
        for j in range(jn):
            bj = b_ref[j] if b.ndim == 3 else b_ref[...]
            part = lax.dot_general(a_ref[:, j * np_:(j + 1) * np_].astype(BF16), bj.astype(BF16),
                                   (((1,), (1,)), ((), ())), preferred_element_type=F32)
            acc = part if acc is None else acc + part
        o_ref[...] = acc.astype(o_ref.dtype)

    if b.ndim == 3:
        b_spec = pl.BlockSpec((jn, tk, np_), lambda i, j: (0, j, 0))
    else:
        b_spec = pl.BlockSpec((tk, n), lambda i, j: (j, 0))
    return pl.pallas_call(
        body, name=name, grid=(m // tm, k // tk),
        in_specs=[pl.BlockSpec((tm, n), lambda i, j: (i, 0)), b_spec],
        out_specs=pl.BlockSpec((tm, tk), lambda i, j: (i, j)),
        out_shape=jax.ShapeDtypeStruct((m, k), out_dtype),
        compiler_params=_params(("parallel", "arbitrary")))(a, b)


def _mm_tn(a, b, like, name):
    m, k = a.shape
    _, n = b.shape
    jn, np_ = _col_blocks(like)
    tk = _tile(k, 512, LANES)
    tn = _fit_tile(np_, 1024, lambda t: 2 * (m * tk * a.dtype.itemsize + m * t * b.dtype.itemsize
                                              + tk * t * like.dtype.itemsize))
    nb = np_ // tn

    def body(a_ref, b_ref, o_ref):
        acc = lax.dot_general(a_ref[...].astype(BF16), b_ref[...].astype(BF16),
                              (((0,), (0,)), ((), ())), preferred_element_type=F32)
        o_ref[...] = acc.astype(o_ref.dtype)

    if like.ndim == 3:
        out_spec = pl.BlockSpec((None, tk, tn), lambda i, j: (j // nb, i, j % nb))
    else:
        out_spec = pl.BlockSpec((tk, tn), lambda i, j: (i, j))
    return pl.pallas_call(
        body, name=name, grid=(k // tk, n // tn),
        in_specs=[pl.BlockSpec((m, tk), lambda i, j: (0, i)), pl.BlockSpec((m, tn), lambda i, j: (0, j))],
        out_specs=out_spec, out_shape=jax.ShapeDtypeStruct(like.shape, like.dtype),
        compiler_params=_params(("parallel", "arbitrary")))(a, b)


def _matmul_op(out_dtype, with_res, tag):
    def fwd_only(a, w, res):
        return _mm_nn(a, w, res if with_res else None, out_dtype, f"mm_{tag}")

    @jax.custom_vjp
    def op(a, w, res):
        return fwd_only(a, w, res)

    def fwd(a, w, res):
        return fwd_only(a, w, res), (a, w, res)

    def bwd(saved, dc):
        a, w, res = saved
        da = _mm_nt(dc, w, a.dtype, f"mm_{tag}_da")
        dw = _mm_tn(a, dc, w, f"mm_{tag}_dw")
        dres = dc.astype(res.dtype) if with_res else jnp.zeros_like(res)
        return da, dw, dres

    op.defvjp(fwd, bwd)
    return op


def matmul(a, w, out_dtype, tag):
    return _matmul_op(out_dtype, False, tag)(a, w, jnp.zeros((), F32))


def matmul_res(a, w, res, tag):
    return _matmul_op(res.dtype, True, tag)(a, w, res)


def _rms_fwd_call(x, gain, groups, out_dtype, name):
    s, w = x.shape
    gw = w // groups
    ts = _tile(s, 512, 128)

    def body(x_ref, g_ref, o_ref):
        for g in range(groups):
            cols = slice(g * gw, (g + 1) * gw)
            xv = x_ref[:, cols].astype(F32)
            r = lax.rsqrt(jnp.mean(xv * xv, axis=1, keepdims=True) + EPS)
            o_ref[:, cols] = (xv * r * g_ref[:, cols]).astype(o_ref.dtype)

    return pl.pallas_call(
        body, name=name, grid=(s // ts,),
        in_specs=[pl.BlockSpec((ts, w), lambda i: (i, 0)), pl.BlockSpec((1, w), lambda i: (0, 0))],
        out_specs=pl.BlockSpec((ts, w), lambda i: (i, 0)),
        out_shape=jax.ShapeDtypeStruct((s, w), out_dtype),
        compiler_params=_params(("parallel",)))(x, gain.reshape(1, w))


def _rms_bwd_call(x, gain, dy, groups, name):
    s, w = x.shape
    gw = w // groups
    ts = _tile(s, 512, 128)

    def body(x_ref, g_ref, dy_ref, dx_ref, dg_ref):
        @pl.when(pl.program_id(0) == 0)
        def _():
            dg_ref[...] = jnp.zeros_like(dg_ref)

        for g in range(groups):
            cols = slice(g * gw, (g + 1) * gw)
            xv = x_ref[:, cols].astype(F32)
            dyv = dy_ref[:, cols].astype(F32)
            r = lax.rsqrt(jnp.mean(xv * xv, axis=1, keepdims=True) + EPS)
            xhat = xv * r
            gdy = dyv * g_ref[:, cols]
            dx = r * (gdy - xhat * jnp.mean(gdy * xhat, axis=1, keepdims=True))
            dx_ref[:, cols] = dx.astype(dx_ref.dtype)
            dg_ref[:, cols] += jnp.sum(dyv * xhat, axis=0, keepdims=True)

    return pl.pallas_call(
        body, name=name, grid=(s // ts,),
        in_specs=[pl.BlockSpec((ts, w), lambda i: (i, 0)), pl.BlockSpec((1, w), lambda i: (0, 0)),
                  pl.BlockSpec((ts, w), lambda i: (i, 0))],
        out_specs=[pl.BlockSpec((ts, w), lambda i: (i, 0)), pl.BlockSpec((1, w), lambda i: (0, 0))],
        out_shape=[jax.ShapeDtypeStruct((s, w), x.dtype), jax.ShapeDtypeStruct((1, w), F32)],
        compiler_params=_params(("arbitrary",)))(x, gain.reshape(1, w), dy)


def rms_norm(x, gain, groups, out_dtype, tag):
    @jax.custom_vjp
    def op(x, gain):
        return _rms_fwd_call(x, gain, groups, out_dtype, f"rms_{tag}")

    def fwd(x, gain):
        return _rms_fwd_call(x, gain, groups, out_dtype, f"rms_{tag}"), (x, gain)

    def bwd(saved, dy):
        x, gain = saved
        dx, dg = _rms_bwd_call(x, gain, dy, groups, f"rms_{tag}_bwd")
        return dx, dg.reshape(gain.shape)

    op.defvjp(fwd, bwd)
    return op(x, gain)


def _rope_call(x, tables, odd_chunks_only, transpose, out_dtype, name):
    s, w = x.shape
    ts = _tile(s, 512, 128)
    cos_t, sin_lo, sin_hi, half = tables
    one_roll = 2 * half == LANES

    def body(x_ref, c_ref, lo_ref, hi_ref, o_ref):
        c, s_lo, s_hi = c_ref[...], lo_ref[...], hi_ref[...]
        for j in range(w // LANES):
            cols = slice(j * LANES, (j + 1) * LANES)
            if odd_chunks_only and j % 2 == 0:
                o_ref[:, cols] = x_ref[:, cols].astype(o_ref.dtype)
                continue
            xv = x_ref[:, cols].astype(F32)
            if one_roll:
                y = xv * c + (pltpu.roll(xv * (s_lo + s_hi), half, 1) if transpose
                              else pltpu.roll(xv, half, 1) * (s_lo + s_hi))
            elif transpose:
                y = xv * c + pltpu.roll(xv * s_lo, LANES - half, 1) + pltpu.roll(xv * s_hi, half, 1)
            else:
                y = xv * c + pltpu.roll(xv, half, 1) * s_lo + pltpu.roll(xv, LANES - half, 1) * s_hi
            o_ref[:, cols] = y.astype(o_ref.dtype)

    row = pl.BlockSpec((ts, LANES), lambda i: (i, 0))
    return pl.pallas_call(
        body, name=name, grid=(s // ts,),
        in_specs=[pl.BlockSpec((ts, w), lambda i: (i, 0)), row, row, row],
        out_specs=pl.BlockSpec((ts, w), lambda i: (i, 0)),
        out_shape=jax.ShapeDtypeStruct((s, w), out_dtype),
        compiler_params=_params(("parallel",)))(x, cos_t, sin_lo, sin_hi)


def rope(x, tables, odd_chunks_only, out_dtype, tag):
    @jax.custom_vjp
    def op(x):
        return _rope_call(x, tables, odd_chunks_only, False, out_dtype, f"rope_{tag}")

    def fwd(x):
        return _rope_call(x, tables, odd_chunks_only, False, out_dtype, f"rope_{tag}"), None

    def bwd(_, dy):
        return (_rope_call(dy, tables, odd_chunks_only, True, x.dtype, f"rope_{tag}_bwd"),)

    op.defvjp(fwd, bwd)
    return op(x)


def rope_tables(seq, dim):
    half = dim // 2
    pos = jnp.arange(seq, dtype=F32)
    inv_freq = ROPE_THETA ** (-jnp.arange(0, dim, 2, dtype=F32) / dim)
    ang = pos[:, None] * inv_freq[None, :]
    cos, sin = jnp.cos(ang), jnp.sin(ang)
    zero = jnp.zeros_like(sin)
    rest = jnp.zeros((seq, LANES - dim), F32)
    return (jnp.concatenate([cos, cos, rest], axis=1), jnp.concatenate([zero, sin, rest], axis=1),
            jnp.concatenate([-sin, zero, rest], axis=1), half)


def _attn_blocks(s):
    return min(256, s), min(512, s)


def _sb_blocks(s):
    b = min(256, s)
    return b, b


def _score_mask(delta, mode):
    if mode != "dilated":
        return None, delta >= 0
    mult = jnp.zeros(delta.shape, F32)
    for window, dilation in DILATED_PAIRS:
        hit = delta <= window
        if dilation > 1:
            hit = hit & ((delta & (dilation - 1)) == 0)
        mult = mult + jnp.where(hit, 1.0, 0.0)
    bias = jnp.where(mult > 2.5, math.log(3.0), jnp.where(mult > 1.5, math.log(2.0), 0.0))
    return bias, (delta >= 0) & (mult > 0.5)


def _kv_range(qi, bq, bk, mode):
    hi = (qi * bq + bq - 1) // bk
    if mode == "dilated":
        lo = jnp.maximum(qi * bq - MAX_WINDOW, 0) // bk
        return lo, lo, hi
    return 0, (qi * bq + 1) // bk, hi


def _attn_fwd_call(q, k, v, cq, ck, heads, scale, mode, name):
    s = q.shape[0]
    dk, dv = q.shape[1] // heads, v.shape[1] // heads
    bq, bk = _attn_blocks(s)
    nq, nk = s // bq, s // bk
    fox = mode == "fox"

    def body(*refs):
        q_ref, k_ref, v_ref = refs[:3]
        o_ref, o32_ref, lse_ref = refs[-3:]
        qi = pl.program_id(1)
        qv = q_ref[...].astype(BF16)
        row_col = (lax.broadcasted_iota(jnp.int32, (bq, bk), 0) - lax.broadcasted_iota(jnp.int32, (bq, bk), 1))
        lo, full, hi = _kv_range(qi, bq, bk, mode)

        def step(masked, kj, carry):
            m, l, acc = carry
            rows = pl.ds(pl.multiple_of(kj * bk, bk), bk)
            kb = k_ref[rows, :].astype(BF16)
            vb = v_ref[rows, :].astype(BF16)
            sc = lax.dot_general(qv, kb, (((1,), (1,)), ((), ())), preferred_element_type=F32) * scale
            if fox:
                sc = sc + refs[3][...] - refs[4][kj]
            if masked:
                bias, mask = _score_mask(row_col + (qi * bq - kj * bk), mode)
                if bias is not None:
                    sc = sc + bias
                sc = jnp.where(mask, sc, NEG_INF)
            m_new = jnp.maximum(m, jnp.max(sc, axis=1, keepdims=True))
            alpha = jnp.exp(m - m_new)
            p = jnp.exp(sc - m_new)
            l = l * alpha + jnp.sum(p, axis=1, keepdims=True)
            acc = acc * alpha + jnp.dot(p.astype(BF16), vb, preferred_element_type=F32)
            return m_new, l, acc

        carry = (jnp.full((bq, 1), NEG_INF, F32), jnp.zeros((bq, 1), F32), jnp.zeros((bq, dv), F32))
        carry = lax.fori_loop(lo, full, functools.partial(step, False), carry)
        m, l, acc = lax.fori_loop(full, hi + 1, functools.partial(step, True), carry)
        out = acc / l
        o_ref[...] = out.astype(o_ref.dtype)
        o32_ref[...] = out
        lse_ref[...] = m + jnp.log(l)

    in_specs = [pl.BlockSpec((bq, dk), lambda h, i: (i, h)), pl.BlockSpec((s, dk), lambda h, i: (0, h)),
                pl.BlockSpec((s, dv), lambda h, i: (0, h))]
    args = [q, k, v]
    if fox:
        in_specs += [pl.BlockSpec((None, bq, 1), lambda h, i: (h, i, 0)),
                     pl.BlockSpec((None, nk, 1, bk), lambda h, i: (h, 0, 0, 0))]
        args += [cq, ck]
    return pl.pallas_call(
        body, name=name, grid=(heads, nq), in_specs=in_specs,
        out_specs=[pl.BlockSpec((bq, dv), lambda h, i: (i, h)), pl.BlockSpec((bq, dv), lambda h, i: (i, h)),
                   pl.BlockSpec((None, bq, 1), lambda h, i: (h, i, 0))],
        out_shape=[jax.ShapeDtypeStruct((s, heads * dv), BF16), jax.ShapeDtypeStruct((s, heads * dv), F32),
                   jax.ShapeDtypeStruct((heads, s, 1), F32)],
        compiler_params=_params(("parallel", "arbitrary")))(*args)


def _attn_bwd_call(q, k, v, cq, ck, o, lse, do, heads, scale, mode, name):
    s = q.shape[0]
    dk, dv = q.shape[1] // heads, v.shape[1] // heads
    bq, bk = _attn_blocks(s)
    nq, nk = s // bq, s // bk
    fox = mode == "fox"
    n_in = 8 if fox else 6

    def body(*refs):
        q_ref, k_ref, v_ref = refs[:3]
        o_ref, lse_ref, do_ref = refs[n_in - 3:n_in]
        outs = refs[n_in:]
        dq_ref, dk_ref, dv_ref = outs[:3]
        scratch = outs[5:] if fox else outs[3:]
        dk_acc, dv_acc = scratch[:2]
        qi = pl.program_id(1)

        @pl.when(qi == 0)
        def _():
            dk_acc[...] = jnp.zeros_like(dk_acc)
            dv_acc[...] = jnp.zeros_like(dv_acc)
            if fox:
                scratch[2][...] = jnp.zeros_like(scratch[2])

        qv = q_ref[...].astype(BF16)
        dov = do_ref[...].astype(BF16)
        delta_o = jnp.sum(dov.astype(F32) * o_ref[...], axis=1, keepdims=True)
        lse = lse_ref[...]
        row_col = (lax.broadcasted_iota(jnp.int32, (bq, bk), 0) - lax.broadcasted_iota(jnp.int32, (bq, bk), 1))
        lo, full, hi = _kv_range(qi, bq, bk, mode)

        def step(masked, kj, carry):
            dq, dcq = carry
            rows = pl.ds(pl.multiple_of(kj * bk, bk), bk)
            kb = k_ref[rows, :].astype(BF16)
            vb = v_ref[rows, :].astype(BF16)
            sc = lax.dot_general(qv, kb, (((1,), (1,)), ((), ())), preferred_element_type=F32) * scale
            if fox:
                sc = sc + refs[3][...] - refs[4][kj]
            if masked:
                bias, mask = _score_mask(row_col + (qi * bq - kj * bk), mode)
                if bias is not None:
                    sc = sc + bias
                sc = jnp.where(mask, sc, NEG_INF)
            p = jnp.exp(sc - lse)
            dv_acc[rows, :] += lax.dot_general(p.astype(BF16), dov, (((0,), (0,)), ((), ())),
                                               preferred_element_type=F32)
            dp = lax.dot_general(dov, vb, (((1,), (1,)), ((), ())), preferred_element_type=F32)
            ds = p * (dp - delta_o)
            dsb = ds.astype(BF16)
            dk_acc[rows, :] += lax.dot_general(dsb, qv, (((0,), (0,)), ((), ())), preferred_element_type=F32)
            if fox:
                scratch[2][kj] -= jnp.sum(ds, axis=0, keepdims=True)
                dcq = dcq + jnp.sum(ds, axis=1, keepdims=True)
            return dq + jnp.dot(dsb, kb, preferred_element_type=F32), dcq

        carry = (jnp.zeros((bq, dk), F32), jnp.zeros((bq, 1), F32))
        carry = lax.fori_loop(lo, full, functools.partial(step, False), carry)
        dq, dcq = lax.fori_loop(full, hi + 1, functools.partial(step, True), carry)
        dq_ref[...] = (dq * scale).astype(dq_ref.dtype)
        if fox:
            outs[4][...] = dcq

        @pl.when(qi == nq - 1)
        def _():
            dk_ref[...] = (dk_acc[...] * scale).astype(dk_ref.dtype)
            dv_ref[...] = dv_acc[...].astype(dv_ref.dtype)
            if fox:
                outs[3][...] = scratch[2][...]

    in_specs = [pl.BlockSpec((bq, dk), lambda h, i: (i, h)), pl.BlockSpec((s, dk), lambda h, i: (0, h)),
                pl.BlockSpec((s, dv), lambda h, i: (0, h))]
    args = [q, k, v]
    if fox:
        in_specs += [pl.BlockSpec((None, bq, 1), lambda h, i: (h, i, 0)),
                     pl.BlockSpec((None, nk, 1, bk), lambda h, i: (h, 0, 0, 0))]
        args += [cq, ck]
    in_specs += [pl.BlockSpec((bq, dv), lambda h, i: (i, h)), pl.BlockSpec((None, bq, 1), lambda h, i: (h, i, 0)),
                 pl.BlockSpec((bq, dv), lambda h, i: (i, h))]
    args += [o, lse, do]
    out_specs = [pl.BlockSpec((bq, dk), lambda h, i: (i, h)), pl.BlockSpec((s, dk), lambda h, i: (0, h)),
                 pl.BlockSpec((s, dv), lambda h, i: (0, h))]
    out_shape = [jax.ShapeDtypeStruct(q.shape, q.dtype), jax.ShapeDtypeStruct(k.shape, k.dtype),
                 jax.ShapeDtypeStruct(v.shape, v.dtype)]
    scratch_shapes = [pltpu.VMEM((s, dk), F32), pltpu.VMEM((s, dv), F32)]
    if fox:
        out_specs.append(pl.BlockSpec((None, nk, 1, bk), lambda h, i: (h, 0, 0, 0)))
        out_shape.append(jax.ShapeDtypeStruct((heads, nk, 1, bk), F32))
        out_specs.append(pl.BlockSpec((None, bq, 1), lambda h, i: (h, i, 0)))
        out_shape.append(jax.ShapeDtypeStruct((heads, s, 1), F32))
        scratch_shapes.append(pltpu.VMEM((nk, 1, bk), F32))
    return pl.pallas_call(
        body, name=name, grid=(heads, nq), in_specs=in_specs, out_specs=out_specs, out_shape=out_shape,
        scratch_shapes=scratch_shapes, compiler_params=_params(("arbitrary", "arbitrary")))(*args)


def attention(q, k, v, cum, heads, scale, mode, tag):
    s = q.shape[0]
    _, bk = _attn_blocks(s)

    def layouts(cum):
        return cum.reshape(heads, s, 1), cum.reshape(heads, s // bk, 1, bk)

    @jax.custom_vjp
    def op(q, k, v, cum):
        cq, ck = layouts(cum)
        return _attn_fwd_call(q, k, v, cq, ck, heads, scale, mode, f"attn_{tag}")[0]

    def fwd(q, k, v, cum):
        cq, ck = layouts(cum)
        o, o32, lse = _attn_fwd_call(q, k, v, cq, ck, heads, scale, mode, f"attn_{tag}")
        return o, (q, k, v, cum, o32, lse)

    def bwd(saved, do):
        q, k, v, cum, o, lse = saved
        cq, ck = layouts(cum)
        res = _attn_bwd_call(q, k, v, cq, ck, o, lse, do, heads, scale, mode, f"attn_{tag}_bwd")
        dcum = res[3].reshape(heads, s) + res[4].reshape(heads, s) if mode == "fox" else jnp.zeros_like(cum)
        return res[0], res[1], res[2], dcum

    op.defvjp(fwd, bwd)
    return op(q, k, v, cum)


def _log_sigmoid(z):
    return jnp.minimum(z, 0.0) - jnp.log(1.0 + jnp.exp(-jnp.abs(z)))


def _tri_dot(x, tri):
    hi = x.astype(BF16)
    lo = (x - hi.astype(F32)).astype(BF16)
    return jnp.dot(hi, tri, preferred_element_type=F32) + jnp.dot(lo, tri, preferred_element_type=F32)


def _sb_call(q, k, v, keep_all, do, heads, scale, backward, name):
    s = q.shape[0]
    d = q.shape[1] // heads
    bq, bk = _sb_blocks(s)
    nq = s // bq

    def body(*refs):
        q_ref, k_ref, v_ref = refs[:3]
        qi = pl.program_id(1)
        qv = q_ref[...].astype(BF16)
        q_pos = qi * bq + lax.broadcasted_iota(jnp.int32, (bq, bk), 0)
        k_col = lax.broadcasted_iota(jnp.int32, (bq, bk), 1)
        r_i = lax.broadcasted_iota(jnp.int32, (bk, bk), 0)
        c_i = lax.broadcasted_iota(jnp.int32, (bk, bk), 1)
        hi = (qi * bq + bq - 1) // bk

        def logits(kj):
            rows = pl.ds(pl.multiple_of(kj * bk, bk), bk)
            kb = k_ref[rows, :].astype(BF16)
            z = lax.dot_general(qv, kb, (((1,), (1,)), ((), ())), preferred_element_type=F32) * scale
            past = (kj * bk + k_col) < q_pos
            ls = _log_sigmoid(z)
            lk = jnp.where(past, ls - z, 0.0)
            return rows, kb, z, past, ls, lk

        if not backward:
            o_ref, keep_ref = refs[3:5]
            after = jnp.where(r_i > c_i, 1.0, 0.0).astype(BF16)

            def step(it, carry):
                keep_right, acc = carry
                rows, _, _, past, ls, lk = logits(hi - it)
                a = jnp.where(past, jnp.exp(ls + _tri_dot(lk, after) + keep_right), 0.0)
                acc = acc + jnp.dot(a.astype(BF16), v_ref[rows, :].astype(BF16), preferred_element_type=F32)
                return keep_right + jnp.sum(lk, axis=1, keepdims=True), acc

            keep, acc = lax.fori_loop(0, hi + 1, step, (jnp.zeros((bq, 1), F32), jnp.zeros((bq, d), F32)))
            o_ref[...] = acc.astype(o_ref.dtype)
            keep_ref[...] = keep
            return

        keep_ref, do_ref, dq_ref, dk_ref, dv_ref, dk_acc, dv_acc = refs[3:]

        @pl.when(qi == 0)
        def _():
            dk_acc[...] = jnp.zeros_like(dk_acc)
            dv_acc[...] = jnp.zeros_like(dv_acc)

        dov = do_ref[...].astype(BF16)
        keep_total = keep_ref[...]
        upto = jnp.where(r_i <= c_i, 1.0, 0.0).astype(BF16)
        before = jnp.where(r_i < c_i, 1.0, 0.0).astype(BF16)

        def step(kj, carry):
            keep_left, g_left, dq = carry
            rows, kb, z, past, ls, lk = logits(kj)
            between = keep_total - (keep_left + _tri_dot(lk, upto))
            a = jnp.where(past, jnp.exp(ls + between), 0.0)
            vb = v_ref[rows, :].astype(BF16)
            da = lax.dot_general(dov, vb, (((1,), (1,)), ((), ())), preferred_element_type=F32)
            g = a * da
            g_before = g_left + _tri_dot(g, before)
            dz = jnp.where(past, g * jnp.exp(ls - z) - jnp.exp(ls) * g_before, 0.0)
            dzb = dz.astype(BF16)
            dk_acc[rows, :] += lax.dot_general(dzb, qv, (((0,), (0,)), ((), ())), preferred_element_type=F32)
            dv_acc[rows, :] += lax.dot_general(a.astype(BF16), dov, (((0,), (0,)), ((), ())),
                                               preferred_element_type=F32)
            dq = dq + jnp.dot(dzb, kb, preferred_element_type=F32)
            return (keep_left + jnp.sum(lk, axis=1, keepdims=True),
                    g_left + jnp.sum(g, axis=1, keepdims=True), dq)

        init = (jnp.zeros((bq, 1), F32), jnp.zeros((bq, 1), F32), jnp.zeros((bq, d), F32))
        _, _, dq = lax.fori_loop(0, hi + 1, step, init)
        dq_ref[...] = (dq * scale).astype(dq_ref.dtype)

        @pl.when(qi == nq - 1)
        def _():
            dk_ref[...] = (dk_acc[...] * scale).astype(dk_ref.dtype)
            dv_ref[...] = dv_acc[...].astype(dv_ref.dtype)

    blk_q = pl.BlockSpec((bq, d), lambda h, i: (i, h))
    blk_kv = pl.BlockSpec((s, d), lambda h, i: (0, h))
    blk_row = pl.BlockSpec((None, bq, 1), lambda h, i: (h, i, 0))
    if not backward:
        return pl.pallas_call(
            body, name=name, grid=(heads, nq), in_specs=[blk_q, blk_kv, blk_kv], out_specs=[blk_q, blk_row],
            out_shape=[jax.ShapeDtypeStruct(q.shape, BF16), jax.ShapeDtypeStruct((heads, s, 1), F32)],
            compiler_params=_params(("parallel", "arbitrary")))(q, k, v)
    return pl.pallas_call(
        body, name=name, grid=(heads, nq), in_specs=[blk_q, blk_kv, blk_kv, blk_row, blk_q],
        out_specs=[blk_q, blk_kv, blk_kv],
        out_shape=[jax.ShapeDtypeStruct(q.shape, q.dtype), jax.ShapeDtypeStruct(k.shape, k.dtype),
                   jax.ShapeDtypeStruct(v.shape, v.dtype)],
        scratch_shapes=[pltpu.VMEM((s, d), F32), pltpu.VMEM((s, d), F32)],
        compiler_params=_params(("arbitrary", "arbitrary")))(q, k, v, keep_all, do)


def stick_breaking(q, k, v, heads, scale, tag):
    @jax.custom_vjp
    def op(q, k, v):
        return _sb_call(q, k, v, None, None, heads, scale, False, f"sb_{tag}")[0]

    def fwd(q, k, v):
        o, keep_all = _sb_call(q, k, v, None, None, heads, scale, False, f"sb_{tag}")
        return o, (q, k, v, keep_all)

    def bwd(saved, do):
        q, k, v, keep_all = saved
        return tuple(_sb_call(q, k, v, keep_all, do, heads, scale, True, f"sb_{tag}_bwd"))

    op.defvjp(fwd, bwd)
    return op(q, k, v)


def _tri_dot3(x, tri):
    hi = x.astype(BF16)
    r1 = x - hi.astype(F32)
    mid = r1.astype(BF16)
    lo = (r1 - mid.astype(F32)).astype(BF16)
    return (jnp.dot(hi, tri, preferred_element_type=F32) + jnp.dot(mid, tri, preferred_element_type=F32)
            + jnp.dot(lo, tri, preferred_element_type=F32))


def _gate_call(f, bias_b, dcum, name):
    heads, r, _ = f.shape
    backward = dcum is not None

    def body(*refs):
        f_ref, b_ref = refs[:2]
        r_i = lax.broadcasted_iota(jnp.int32, (LANES, LANES), 0)
        c_i = lax.broadcasted_iota(jnp.int32, (LANES, LANES), 1)
        x = f_ref[...] + b_ref[...]
        if not backward:
            o_ref, ls_ref = refs[2:]
            ls_ref[...] = _log_sigmoid(x)
            o_ref[...] = _tri_dot3(ls_ref[...], jnp.where(r_i <= c_i, 1.0, 0.0).astype(BF16))

            def row(i, carry):
                o_ref[pl.ds(i, 1), :] = o_ref[pl.ds(i, 1), :] + carry
                return carry + jnp.sum(ls_ref[pl.ds(i, 1), :], axis=1, keepdims=True)

            lax.fori_loop(0, r, row, jnp.zeros((1, 1), F32))
            return

        dc_ref, df_ref, db_ref, acc_ref = refs[2:]
        acc_ref[...] = _tri_dot3(dc_ref[...], jnp.where(r_i >= c_i, 1.0, 0.0).astype(BF16))

        def row(it, carry):
            i = r - 1 - it
            acc_ref[pl.ds(i, 1), :] = acc_ref[pl.ds(i, 1), :] + carry
            return carry + jnp.sum(dc_ref[pl.ds(i, 1), :], axis=1, keepdims=True)

        lax.fori_loop(0, r, row, jnp.zeros((1, 1), F32))
        df = acc_ref[...] * jnp.exp(_log_sigmoid(-x))
        df_ref[...] = df
        lane = lax.broadcasted_iota(jnp.int32, (1, LANES), 1)
        db_ref[...] = jnp.where(lane == 0, jnp.sum(df), 0.0)

    blk = pl.BlockSpec((None, r, LANES), lambda h: (h, 0, 0))
    blk_b = pl.BlockSpec((None, 1, LANES), lambda h: (h, 0, 0))
    if not backward:
        return pl.pallas_call(
            body, name=name, grid=(heads,), in_specs=[blk, blk_b], out_specs=blk,
            out_shape=jax.ShapeDtypeStruct(f.shape, F32), scratch_shapes=[pltpu.VMEM((r, LANES), F32)],
            compiler_params=_params(("parallel",)))(f, bias_b)
    return pl.pallas_call(
        body, name=name, grid=(heads,), in_specs=[blk, blk_b, blk], out_specs=[blk, blk_b],
        out_shape=[jax.ShapeDtypeStruct(f.shape, F32), jax.ShapeDtypeStruct(bias_b.shape, F32)],
        scratch_shapes=[pltpu.VMEM((r, LANES), F32)],
        compiler_params=_params(("parallel",)))(f, bias_b, dcum)


def forget_gate_cumsum(f, bias_b, tag):
    @jax.custom_vjp
    def op(f, bias_b):
        return _gate_call(f, bias_b, None, f"gate_{tag}")

    def fwd(f, bias_b):
        return _gate_call(f, bias_b, None, f"gate_{tag}"), (f, bias_b)

    def bwd(saved, dcum):
        return tuple(_gate_call(saved[0], saved[1], dcum, f"gate_{tag}_bwd"))

    op.defvjp(fwd, bwd)
    return op(f, bias_b)


def _swiglu_call(g, u, dact, name):
    s, n = g.shape
    ts = _tile(s, 256, 128)
    backward = dact is not None

    def body(*refs):
        gv = refs[0][...].astype(F32)
        uv = refs[1][...].astype(F32)
        sig = 1.0 / (1.0 + jnp.exp(-gv))
        if not backward:
            refs[2][...] = (gv * sig * uv).astype(refs[2].dtype)
            return
        dv = refs[2][...].astype(F32)
        refs[3][...] = (dv * uv * sig * (1.0 + gv * (1.0 - sig))).astype(refs[3].dtype)
        refs[4][...] = (dv * gv * sig).astype(refs[4].dtype)

    blk = pl.BlockSpec((ts, n), lambda i: (i, 0))
    shape = jax.ShapeDtypeStruct((s, n), g.dtype)
    if not backward:
        return pl.pallas_call(body, name=name, grid=(s // ts,), in_specs=[blk, blk], out_specs=blk,
                              out_shape=shape, compiler_params=_params(("parallel",)))(g, u)
    return pl.pallas_call(body, name=name, grid=(s // ts,), in_specs=[blk, blk, blk], out_specs=[blk, blk],
                          out_shape=[shape, shape], compiler_params=_params(("parallel",)))(g, u, dact)


def swiglu(g, u, tag):
    @jax.custom_vjp
    def op(g, u):
        return _swiglu_call(g, u, None, f"swiglu_{tag}")

    def fwd(g, u):
        return _swiglu_call(g, u, None, f"swiglu_{tag}"), (g, u)

    def bwd(saved, dact):
        return tuple(_swiglu_call(saved[0], saved[1], dact, f"swiglu_{tag}_bwd"))

    op.defvjp(fwd, bwd)
    return op(g, u)


def final_norm_loss(x, gain, target):
    s, d = x.shape
    ts = _tile(s, 256, 128)

    def body(x_ref, g_ref, t_ref, sq_ref, dx_ref, dg_ref):
        @pl.when(pl.program_id(0) == 0)
        def _():
            sq_ref[...] = jnp.zeros_like(sq_ref)
            dg_ref[...] = jnp.zeros_like(dg_ref)

        xv = x_ref[...]
        r = lax.rsqrt(jnp.mean(xv * xv, axis=1, keepdims=True) + EPS)
        xhat = xv * r
        err = xhat * g_ref[...] - t_ref[...]
        sq_ref[...] += jnp.sum(err * err)
        dy = err * (1.0 / d)
        gdy = dy * g_ref[...]
        dx_ref[...] = r * (gdy - xhat * jnp.mean(gdy * xhat, axis=1, keepdims=True))
        dg_ref[...] += jnp.sum(dy * xhat, axis=0, keepdims=True)

    blk = pl.BlockSpec((ts, d), lambda i: (i, 0))
    row = pl.BlockSpec((1, d), lambda i: (0, 0))
    return pl.pallas_call(
        body, name="final_norm_loss", grid=(s // ts,), in_specs=[blk, row, blk],
        out_specs=[pl.BlockSpec((1, LANES), lambda i: (0, 0)), blk, row],
        out_shape=[jax.ShapeDtypeStruct((1, LANES), F32), jax.ShapeDtypeStruct((s, d), F32),
                   jax.ShapeDtypeStruct((1, d), F32)],
        compiler_params=_params(("arbitrary",)))(x, gain.reshape(1, d), target)


def adamw(w, g, m, v, name):
    rows, cols = w.shape
    tr = _tile(rows, 256, 8)
    c1 = 1.0 - ADAM_B1 ** ADAM_STEP
    c2 = 1.0 - ADAM_B2 ** ADAM_STEP

    def body(w_ref, g_ref, m_ref, v_ref, d_ref, nm_ref, nv_ref):
        gv = g_ref[...]
        m_new = ADAM_B1 * m_ref[...] + (1.0 - ADAM_B1) * gv
        v_new = ADAM_B2 * v_ref[...] + (1.0 - ADAM_B2) * (gv * gv)
        d_ref[...] = -ADAM_LR * ((m_new / c1) / (jnp.sqrt(v_new / c2) + ADAM_EPS) + ADAM_WD * w_ref[...])
        nm_ref[...] = m_new
        nv_ref[...] = v_new

    blk = pl.BlockSpec((tr, cols), lambda i: (i, 0))
    shape = jax.ShapeDtypeStruct((rows, cols), F32)
    return pl.pallas_call(body, name=name, grid=(rows // tr,), in_specs=[blk] * 4, out_specs=[blk] * 3,
                          out_shape=[shape] * 3, compiler_params=_params(("parallel",)))(w, g, m, v)


ANY = pl.BlockSpec(memory_space=pl.ANY)


def _position():
    return lax.axis_index("x"), lax.axis_index("y"), lax.axis_index("c")


def gather_chips(shards, name):
    n = len(shards)

    def body(*refs):
        ins, outs = refs[:n], refs[n:2 * n]
        send_sems, recv_sems, local_sems = refs[2 * n:]
        x, y, c = _position()
        chips = [(1 - x, y), (x, 1 - y), (1 - x, 1 - y)]

        def copy(a, k, slot):
            return pltpu.make_async_remote_copy(
                src_ref=ins[a], dst_ref=outs[a].at[slot], send_sem=send_sems.at[a, k], recv_sem=recv_sems.at[a, k],
                device_id=(chips[k][0], chips[k][1], c), device_id_type=MESH)

        local = [pltpu.make_async_copy(ins[a], outs[a].at[2 * x + y], local_sems.at[a]) for a in range(n)]
        sends = [copy(a, k, 2 * x + y) for a in range(n) for k in range(3)]
        for cp in local + sends:
            cp.start()
        for a in range(n):
            for k in range(3):
                copy(a, k, 2 * chips[k][0] + chips[k][1]).wait_recv()
        for cp in sends:
            cp.wait_send()
        for cp in local:
            cp.wait()

    return pl.pallas_call(
        body, name=name, in_specs=[ANY] * n, out_specs=[ANY] * n,
        out_shape=[jax.ShapeDtypeStruct((4,) + s.shape, s.dtype) for s in shards],
        scratch_shapes=[pltpu.SemaphoreType.DMA((n, 3)), pltpu.SemaphoreType.DMA((n, 3)),
                        pltpu.SemaphoreType.DMA((n,))])(*shards)


def exchange_all(parts, name):
    n = len(parts)

    def body(*refs):
        ins, outs = refs[:n], refs[n:2 * n]
        send_sems, recv_sems, local_sems = refs[2 * n:]
        x, y, c = _position()
        me = 4 * x + 2 * y + c

        def copy(a, k, to_me):
            px, py, pc = (1 - x if k & 4 else x, 1 - y if k & 2 else y, 1 - c if k & 1 else c)
            slot = 4 * px + 2 * py + pc if to_me else me
            return pltpu.make_async_remote_copy(
                src_ref=ins[a].at[2 * px + py, pc], dst_ref=outs[a].at[slot],
                send_sem=send_sems.at[a, k - 1], recv_sem=recv_sems.at[a, k - 1],
                device_id=(px, py, pc), device_id_type=MESH)

        local = [pltpu.make_async_copy(ins[a].at[2 * x + y, c], outs[a].at[me], local_sems.at[a]) for a in range(n)]
        sends = [copy(a, k, False) for a in range(n) for k in range(1, 8)]
        for cp in local + sends:
            cp.start()
        for a in range(n):
            for k in range(1, 8):
                copy(a, k, True).wait_recv()
        for cp in sends:
            cp.wait_send()
        for cp in local:
            cp.wait()

    return pl.pallas_call(
        body, name=name, in_specs=[ANY] * n, out_specs=[ANY] * n,
        out_shape=[jax.ShapeDtypeStruct((8,) + p.shape[2:], p.dtype) for p in parts],
        scratch_shapes=[pltpu.SemaphoreType.DMA((n, 7)), pltpu.SemaphoreType.DMA((n, 7)),
                        pltpu.SemaphoreType.DMA((n,))])(*parts)


def broadcast_all(block, name):
    r, lanes = block.shape

    def body(b_ref, out_ref, send_sems, recv_sems, local_sem):
        x, y, c = _position()
        me = 4 * x + 2 * y + c
        mine = pltpu.make_async_copy(b_ref, out_ref.at[me], local_sem)
        mine.start()

        def copy(k, to_me):
            px, py, pc = (1 - x if k & 4 else x, 1 - y if k & 2 else y, 1 - c if k & 1 else c)
            slot = 4 * px + 2 * py + pc if to_me else me
            return pltpu.make_async_remote_copy(
                src_ref=b_ref, dst_ref=out_ref.at[slot], send_sem=send_sems.at[k - 1],
                recv_sem=recv_sems.at[k - 1], device_id=(px, py, pc), device_id_type=MESH)

        sends = [copy(k, False) for k in range(1, 8)]
        for cp in sends:
            cp.start()
        for k in range(1, 8):
            copy(k, True).wait_recv()
        for cp in sends:
            cp.wait_send()
        mine.wait()

    return pl.pallas_call(
        body, name=name, in_specs=[ANY], out_specs=ANY,
        out_shape=jax.ShapeDtypeStruct((8, r, lanes), block.dtype),
        scratch_shapes=[pltpu.SemaphoreType.DMA((7,)), pltpu.SemaphoreType.DMA((7,)), pltpu.SemaphoreType.DMA])(block)


SWAP_CHUNKS = 4


def swap_halves(halves, name):
    n = len(halves)
    chunk_rows = [h.shape[0] // SWAP_CHUNKS for h in halves]

    def body(*refs):
        ins, outs = refs[:n], refs[n:2 * n]
        send_sems, recv_sems, local_sems = refs[2 * n:]
        x, y, c = _position()

        def copy(a, j, slot):
            rows = pl.ds(j * chunk_rows[a], chunk_rows[a])
            return pltpu.make_async_remote_copy(
                src_ref=ins[a].at[rows], dst_ref=outs[a].at[slot, rows], send_sem=send_sems.at[a, j],
                recv_sem=recv_sems.at[a, j], device_id=(x, y, 1 - c), device_id_type=MESH)

        local = [pltpu.make_async_copy(ins[a], outs[a].at[c], local_sems.at[a]) for a in range(n)]
        sends = [copy(a, j, c) for a in range(n) for j in range(SWAP_CHUNKS)]
        for cp in local + sends:
            cp.start()
        for a in range(n):
            for j in range(SWAP_CHUNKS):
                copy(a, j, 1 - c).wait_recv()
        for cp in sends:
            cp.wait_send()
        for cp in local:
            cp.wait()

    return pl.pallas_call(
        body, name=name, in_specs=[ANY] * n, out_specs=[ANY] * n,
        out_shape=[jax.ShapeDtypeStruct((2,) + h.shape, h.dtype) for h in halves],
        scratch_shapes=[pltpu.SemaphoreType.DMA((n, SWAP_CHUNKS)), pltpu.SemaphoreType.DMA((n, SWAP_CHUNKS)),
                        pltpu.SemaphoreType.DMA((n,))])(*halves)


def sum_slots(stack, name):
    n, r, lanes = stack.shape
    tr = _tile(r, max(16, (1 << 22) // (n * lanes * stack.dtype.itemsize) // 16 * 16), 16)

    def body(s_ref, o_ref):
        acc = s_ref[0].astype(F32)
        for i in range(1, n):
            acc = acc + s_ref[i].astype(F32)
        o_ref[...] = acc

    return pl.pallas_call(
        body, name=name, grid=(r // tr,), in_specs=[pl.BlockSpec((n, tr, lanes), lambda i: (0, i, 0))],
        out_specs=pl.BlockSpec((tr, lanes), lambda i: (i, 0)), out_shape=jax.ShapeDtypeStruct((r, lanes), F32),
        compiler_params=_params(("parallel",)))(stack)


def _pack(arrays, dtype, row_align):
    flat = jnp.concatenate([a.reshape(-1).astype(dtype) for a in arrays])
    rows = -(-flat.shape[0] // LANES)
    rows = -(-rows // row_align) * row_align
    return jnp.pad(flat, (0, rows * LANES - flat.shape[0])).reshape(rows, LANES)


def _unpack(flat, shapes):
    flat = flat.reshape(flat.shape[:-2] + (-1,))
    out, at = [], 0
    for shp in shapes:
        n = math.prod(shp)
        out.append(flat[..., at:at + n].reshape(flat.shape[:-1] + tuple(shp)))
        at += n
    return out


def _pad_cols(w):
    return jnp.pad(w, ((0, 0), (0, -w.shape[1] % LANES)))


def split_columns(raw, width, widths, dtypes):
    wp = raw.shape[1] * width // sum(widths)
    jn = raw.shape[1] // wp

    def cut(raw):
        whole = jnp.concatenate([raw[:, j * wp:j * wp + width] for j in range(jn)], axis=1)
        out, at = [], 0
        for n, dt in zip(widths, dtypes):
            out.append(whole[:, at:at + n].astype(dt))
            at += n
        return tuple(out)

    @jax.custom_vjp
    def op(raw):
        return cut(raw)

    def fwd(raw):
        return cut(raw), None

    def bwd(_, cts):
        whole = jnp.concatenate([ct.astype(BF16) for ct in cts], axis=1)
        gap = jnp.zeros((whole.shape[0], wp - width), BF16)
        blocks = []
        for j in range(jn):
            blocks += [whole[:, j * width:(j + 1) * width], gap]
        return (jnp.concatenate(blocks, axis=1).astype(raw.dtype),)

    op.defvjp(fwd, bwd)
    return op(raw)


def _layer(x, mats, smalls, tables, widths, heads, tag):
    s, d_model = x.shape
    gw = heads * HEAD_DIM
    w_in, w_uq, w_ukv, w_out, w_gate, w_up, w_down = mats
    attn_norm, q_norm, kv_norm, f_bias, group_norm, ffn_norm = smalls
    rope_full, rope_mla = tables
    no_cum = jnp.zeros((heads, s), F32)

    h = rms_norm(x, attn_norm, 1, BF16, f"attn_{tag}")
    raw = matmul(h, w_in, F32, f"in_{tag}")
    sizes = [Q_LORA, KV_LORA, QK_ROPE, 2 * gw, gw, gw, gw, gw, heads, gw, gw, gw]
    kinds = [F32, F32, F32, F32, BF16, BF16, BF16, BF16, F32, BF16, BF16, BF16]
    q_lat, kv_lat, k_rope, qk_b, v_b, q_c, k_c, v_c, f_logit, q_d, k_d, v_d = split_columns(
        raw, widths[0], sizes, kinds)

    q_all = matmul(rms_norm(q_lat, q_norm, 1, BF16, f"qlat_{tag}"), w_uq, BF16, f"uq_{tag}")
    kv_all = matmul(rms_norm(kv_lat, kv_norm, 1, BF16, f"kvlat_{tag}"), w_ukv, BF16, f"ukv_{tag}")
    q_a = rope(q_all, rope_mla, True, BF16, f"qpe_{tag}")
    k_pe = rope(_pad_cols(k_rope), rope_mla, False, BF16, f"kpe_{tag}")
    kv4 = kv_all.reshape(s, heads, 2, LANES)
    k_a = jnp.stack([kv4[:, :, 0], jnp.broadcast_to(k_pe[:, None, :], (s, heads, LANES))], axis=2).reshape(s, 2 * gw)
    out_a = attention(q_a, k_a, kv4[:, :, 1].reshape(s, gw), no_cum, heads, (QK_NOPE + QK_ROPE) ** -0.5,
                      "causal", f"a_{tag}")

    qk_b = rope(qk_b, rope_full, False, BF16, f"qkb_{tag}")
    out_b = attention(qk_b[:, :gw], qk_b[:, gw:], v_b, no_cum, heads, HEAD_DIM ** -0.5, "dilated", f"b_{tag}")

    f_rows = f_logit.T.reshape(heads, s // LANES, LANES)
    bias_b = jnp.broadcast_to(f_bias[:, None, None], (heads, 1, LANES))
    cum = forget_gate_cumsum(f_rows, bias_b, tag).reshape(heads, s)
    out_c = attention(q_c, k_c, v_c, cum, heads, HEAD_DIM ** -0.5, "fox", f"c_{tag}")

    out_d = stick_breaking(q_d, k_d, v_d, heads, HEAD_DIM ** -0.5, f"d_{tag}")

    groups = jnp.concatenate([out_a, out_b, out_c, out_d], axis=1)
    x = matmul_res(rms_norm(groups, group_norm, 4, BF16, f"group_{tag}"), w_out.reshape(-1, d_model), x, f"out_{tag}")

    h2 = rms_norm(x, ffn_norm, 1, BF16, f"ffn_{tag}")
    act = swiglu(matmul(h2, w_gate, BF16, f"gate_{tag}"), matmul(h2, w_up, BF16, f"up_{tag}"), tag)
    return matmul_res(act, w_down.reshape(-1, d_model), x, f"down_{tag}")


def kernel(x, attn_norm, w_in, mla_q_norm, w_uq, mla_kv_norm, w_ukv, fox_forget_bias, group_norm, w_out, ffn_norm, w_gate, w_up, w_down, final_norm, loss_target, m_attn_norm, m_w_in, m_mla_q_norm, m_w_uq, m_mla_kv_norm, m_w_ukv, m_fox_forget_bias, m_group_norm, m_w_out, m_ffn_norm, m_w_gate, m_w_up, m_w_down, m_final_norm, v_attn_norm, v_w_in, v_mla_q_norm, v_w_uq, v_mla_kv_norm, v_w_ukv, v_fox_forget_bias, v_group_norm, v_w_out, v_ffn_norm, v_w_gate, v_w_up, v_w_down, v_final_norm):
    depth = w_in.shape[0]
    _, s, d_model = x.shape
    heads = d_model // 4 // HEAD_DIM
    big = [w_in, w_uq, w_ukv, w_out, w_gate, w_up, w_down]
    big_m = [m_w_in, m_w_uq, m_w_ukv, m_w_out, m_w_gate, m_w_up, m_w_down]
    big_v = [v_w_in, v_w_uq, v_w_ukv, v_w_out, v_w_gate, v_w_up, v_w_down]
    small = [attn_norm, mla_q_norm, mla_kv_norm, fox_forget_bias, group_norm, ffn_norm]
    small_m = [m_attn_norm, m_mla_q_norm, m_mla_kv_norm, m_fox_forget_bias, m_group_norm, m_ffn_norm, m_final_norm]
    small_v = [v_attn_norm, v_mla_q_norm, v_mla_kv_norm, v_fox_forget_bias, v_group_norm, v_ffn_norm, v_final_norm]
    widths = [w.shape[2] for w in big]
    tables = (rope_tables(s, HEAD_DIM), rope_tables(s, QK_ROPE))

    gathered = [gather_chips([_pad_cols(w[l]).astype(BF16) for w in big], f"gather_{l}") for l in range(depth)]

    def trunk(gathered, small, x):
        for l in range(depth):
            x = _layer(x, gathered[l], [p[l] for p in small], tables, widths, heads, str(l))
        return x

    x_out, vjp = jax.vjp(trunk, gathered, small, x[0])
    sq, dx_out, d_final = final_norm_loss(x_out, final_norm, loss_target[0])
    loss = 0.5 / d_model * lax.psum(sq[0, 0], ("x", "y", "c"))
    d_gathered, d_small, dx = vjp(dx_out)

    big_grads = []
    for l in range(depth):
        parts = [g.reshape(4, 2, g.shape[1] // 2, g.shape[2]) for g in d_gathered[l]]
        halves = [sum_slots(r, f"sum_{l}_{i}") for i, r in enumerate(exchange_all(parts, f"scatter_{l}"))]
        full = [f.reshape(-1, f.shape[2])[:, :wd] for f, wd in zip(swap_halves(halves, f"swap_{l}"), widths)]
        big_grads.append(full)
    big_g = [jnp.stack([big_grads[l][i] for l in range(depth)]) for i in range(len(big))]

    small_all = small + [final_norm]
    packed = _pack(d_small + [d_final], F32, 8)
    small_sum = sum_slots(broadcast_all(packed, "small_gather"), "small_sum")
    small_g = _unpack(small_sum, [p.shape for p in small_all])

    def update(w, g, m, v, name):
        shp = w.shape
        two_d = (-1, shp[-1])
        return [o.reshape(shp) for o in adamw(w.reshape(two_d), g.reshape(two_d), m.reshape(two_d),
                                              v.reshape(two_d), name)]

    big_u = [update(w, g, m, v, f"adamw_big{i}") for i, (w, g, m, v) in enumerate(zip(big, big_g, big_m, big_v))]
    small_u = adamw(_pack(small_all, F32, 8), small_sum, _pack(small_m, F32, 8), _pack(small_v, F32, 8), "adamw_small")
    small_u = [_unpack(u, [p.shape for p in small_all]) for u in small_u]

    def ordered(bigs, smalls):
        a_n, q_n, kv_n, f_b, g_n, f_n, fin = smalls
        wi, uq, ukv, wo, wg, wu, wd = bigs
        return [a_n, wi, q_n, uq, kv_n, ukv, f_b, g_n, wo, f_n, wg, wu, wd, fin]

    grads = ordered(big_g, small_g)
    deltas = ordered([u[0] for u in big_u], small_u[0])
    new_m = ordered([u[1] for u in big_u], small_u[1])
    new_v = ordered([u[2] for u in big_u], small_u[2])
    return (loss, dx[None], *grads, *deltas, *new_m, *new_v)
```

```python
import functools
import math

import jax
import jax.numpy as jnp
from jax import lax
from jax.experimental import pallas as pl
from jax.experimental.pallas import tpu as pltpu

F32 = jnp.float32
BF16 = jnp.bfloat16
MESH = pl.DeviceIdType.MESH

HEAD_DIM = 128
Q_LORA = 512
KV_LORA = 512
QK_NOPE = 128
QK_ROPE = 64
DILATED_PAIRS = ((128, 1), (512, 4), (2048, 16))
MAX_WINDOW = max(w for w, _ in DILATED_PAIRS)
ROPE_THETA = 10000.0
EPS = 1e-6
NEG_INF = -1e30
ADAM_LR = 0.001
ADAM_B1 = 0.9
ADAM_B2 = 0.999
ADAM_EPS = 1e-08
ADAM_WD = 0.01
ADAM_STEP = 10

LANES = 128
VMEM_LIMIT_BYTES = 56 * 1024 * 1024


def _params(semantics=None):
    return pltpu.CompilerParams(dimension_semantics=semantics, vmem_limit_bytes=VMEM_LIMIT_BYTES)


MATMUL_VMEM_BUDGET = 40 * 1024 * 1024
MAX_TILE_COLS = 1536


def _fit_tile(n, cap, estimate):
    t = _tile(n, cap, LANES)
    while estimate(t) > MATMUL_VMEM_BUDGET and t > LANES:
        t = _tile(n, t - LANES, LANES)
    return t


def _tile(n, cap, align):
    if n <= cap:
        return n
    t = cap - cap % align
    while t >= align:
        if n % t == 0:
            return t
        t -= align
    raise ValueError(f"no tile for {n} under {cap}")


def _col_blocks(b):
    return (b.shape[0], b.shape[2]) if b.ndim == 3 else (1, b.shape[1])


def _mm_nn(a, b, res, out_dtype, name):
    m, k = a.shape
    jn, np_ = _col_blocks(b)
    n = jn * np_
    tm = _tile(m, 1024, 128)
    res_bytes = 0 if res is None else res.dtype.itemsize
    tn = _fit_tile(np_, MAX_TILE_COLS, lambda t: 2 * (tm * k * a.dtype.itemsize + k * t * b.dtype.itemsize
                                                       + tm * t * (jnp.dtype(out_dtype).itemsize + res_bytes)))
    nb = np_ // tn

    def body(*refs):
        a_ref, b_ref = refs[0], refs[1]
        o_ref = refs[-1]
        acc = jnp.dot(a_ref[...].astype(BF16), b_ref[...].astype(BF16), preferred_element_type=F32)
        if res is not None:
            acc = acc + refs[2][...].astype(F32)
        o_ref[...] = acc.astype(o_ref.dtype)

    if b.ndim == 3:
        b_spec = pl.BlockSpec((None, k, tn), lambda i, j: (j // nb, 0, j % nb))
    else:
        b_spec = pl.BlockSpec((k, tn), lambda i, j: (0, j))
    in_specs = [pl.BlockSpec((tm, k), lambda i, j: (i, 0)), b_spec]
    args = [a, b]
    if res is not None:
        in_specs.append(pl.BlockSpec((tm, tn), lambda i, j: (i, j)))
        args.append(res)
    return pl.pallas_call(
        body, name=name, grid=(m // tm, n // tn), in_specs=in_specs,
        out_specs=pl.BlockSpec((tm, tn), lambda i, j: (i, j)),
        out_shape=jax.ShapeDtypeStruct((m, n), out_dtype),
        compiler_params=_params(("parallel", "arbitrary")))(*args)


def _mm_nt(a, b, out_dtype, name):
    m, n = a.shape
    jn, np_ = _col_blocks(b)
    k = b.shape[-2]
    tm = _tile(m, 1024, 128)
    tk = _fit_tile(k, 1024, lambda t: 2 * (tm * n * a.dtype.itemsize + t * n * b.dtype.itemsize
                                            + tm * t * jnp.dtype(out_dtype).itemsize))

    def body(a_ref, b_ref, o_ref):
        acc = None
        for j in range(jn):
            bj = b_ref[j] if b.ndim == 3 else b_ref[...]
            part = lax.dot_general(a_ref[:, j * np_:(j + 1) * np_].astype(BF16), bj.astype(BF16),
                                   (((1,), (1,)), ((), ())), preferred_element_type=F32)
            acc = part if acc is None else acc + part
        o_ref[...] = acc.astype(o_ref.dtype)

    if b.ndim == 3:
        b_spec = pl.BlockSpec((jn, tk, np_), lambda i, j: (0, j, 0))
    else:
        b_spec = pl.BlockSpec((tk, n), lambda i, j: (j, 0))
    return pl.pallas_call(
        body, name=name, grid=(m // tm, k // tk),
        in_specs=[pl.BlockSpec((tm, n), lambda i, j: (i, 0)), b_spec],
        out_specs=pl.BlockSpec((tm, tk), lambda i, j: (i, j)),
        out_shape=jax.ShapeDtypeStruct((m, k), out_dtype),
        compiler_params=_params(("parallel", "arbitrary")))(a, b)


def _mm_tn(a, b, like, name):
    m, k = a.shape
    _, n = b.shape
    jn, np_ = _col_blocks(like)
    tk = _tile(k, 512, LANES)
    tn = _fit_tile(np_, MAX_TILE_COLS, lambda t: 2 * (m * tk * a.dtype.itemsize + m * t * b.dtype.itemsize
                                                       + tk * t * like.dtype.itemsize))
    nb = np_ // tn

    def body(a_ref, b_ref, o_ref):
        acc = lax.dot_general(a_ref[...].astype(BF16), b_ref[...].astype(BF16),
                              (((0,), (0,)), ((), ())), preferred_element_type=F32)
        o_ref[...] = acc.astype(o_ref.dtype)

    if like.ndim == 3:
        out_spec = pl.BlockSpec((None, tk, tn), lambda i, j: (j // nb, i, j % nb))
    else:
        out_spec = pl.BlockSpec((tk, tn), lambda i, j: (i, j))
    return pl.pallas_call(
        body, name=name, grid=(k // tk, n // tn),
        in_specs=[pl.BlockSpec((m, tk), lambda i, j: (0, i)), pl.BlockSpec((m, tn), lambda i, j: (0, j))],
        out_specs=out_spec, out_shape=jax.ShapeDtypeStruct(like.shape, like.dtype),
        compiler_params=_params(("parallel", "arbitrary")))(a, b)


def _matmul_op(out_dtype, with_res, tag):
    def fwd_only(a, w, res):
        return _mm_nn(a, w, res if with_res else None, out_dtype, f"mm_{tag}")

    @jax.custom_vjp
    def op(a, w, res):
        return fwd_only(a, w, res)

    def fwd(a, w, res):
        return fwd_only(a, w, res), (a, w, res)

    def bwd(saved, dc):
        a, w, res = saved
        da = _mm_nt(dc, w, a.dtype, f"mm_{tag}_da")
        dw = _mm_tn(a, dc, w, f"mm_{tag}_dw")
        dres = dc.astype(res.dtype) if with_res else jnp.zeros_like(res)
        return da, dw, dres

    op.defvjp(fwd, bwd)
    return op


def matmul(a, w, out_dtype, tag):
    return _matmul_op(out_dtype, False, tag)(a, w, jnp.zeros((), F32))


def matmul_res(a, w, res, tag):
    return _matmul_op(res.dtype, True, tag)(a, w, res)


def _rms_fwd_call(x, gain, groups, out_dtype, name):
    s, w = x.shape
    gw = w // groups
    ts = _tile(s, 512, 128)

    def body(x_ref, g_ref, o_ref):
        for g in range(groups):
            cols = slice(g * gw, (g + 1) * gw)
            xv = x_ref[:, cols].astype(F32)
            r = lax.rsqrt(jnp.mean(xv * xv, axis=1, keepdims=True) + EPS)
            o_ref[:, cols] = (xv * r * g_ref[:, cols]).astype(o_ref.dtype)

    return pl.pallas_call(
        body, name=name, grid=(s // ts,),
        in_specs=[pl.BlockSpec((ts, w), lambda i: (i, 0)), pl.BlockSpec((1, w), lambda i: (0, 0))],
        out_specs=pl.BlockSpec((ts, w), lambda i: (i, 0)),
        out_shape=jax.ShapeDtypeStruct((s, w), out_dtype),
        compiler_params=_params(("parallel",)))(x, gain.reshape(1, w))


def _rms_bwd_call(x, gain, dy, groups, name):
    s, w = x.shape
    gw = w // groups
    ts = _tile(s, 512, 128)

    def body(x_ref, g_ref, dy_ref, dx_ref, dg_ref):
        @pl.when(pl.program_id(0) == 0)
        def _():
            dg_ref[...] = jnp.zeros_like(dg_ref)

        for g in range(groups):
            cols = slice(g * gw, (g + 1) * gw)
            xv = x_ref[:, cols].astype(F32)
            dyv = dy_ref[:, cols].astype(F32)
            r = lax.rsqrt(jnp.mean(xv * xv, axis=1, keepdims=True) + EPS)
            xhat = xv * r
            gdy = dyv * g_ref[:, cols]
            dx = r * (gdy - xhat * jnp.mean(gdy * xhat, axis=1, keepdims=True))
            dx_ref[:, cols] = dx.astype(dx_ref.dtype)
            dg_ref[:, cols] += jnp.sum(dyv * xhat, axis=0, keepdims=True)

    return pl.pallas_call(
        body, name=name, grid=(s // ts,),
        in_specs=[pl.BlockSpec((ts, w), lambda i: (i, 0)), pl.BlockSpec((1, w), lambda i: (0, 0)),
                  pl.BlockSpec((ts, w), lambda i: (i, 0))],
        out_specs=[pl.BlockSpec((ts, w), lambda i: (i, 0)), pl.BlockSpec((1, w), lambda i: (0, 0))],
        out_shape=[jax.ShapeDtypeStruct((s, w), x.dtype), jax.ShapeDtypeStruct((1, w), F32)],
        compiler_params=_params(("arbitrary",)))(x, gain.reshape(1, w), dy)


def rms_norm(x, gain, groups, out_dtype, tag):
    @jax.custom_vjp
    def op(x, gain):
        return _rms_fwd_call(x, gain, groups, out_dtype, f"rms_{tag}")

    def fwd(x, gain):
        return _rms_fwd_call(x, gain, groups, out_dtype, f"rms_{tag}"), (x, gain)

    def bwd(saved, dy):
        x, gain = saved
        dx, dg = _rms_bwd_call(x, gain, dy, groups, f"rms_{tag}_bwd")
        return dx, dg.reshape(gain.shape)

    op.defvjp(fwd, bwd)
    return op(x, gain)


def _rope_call(x, tables, odd_chunks_only, transpose, out_dtype, name):
    s, w = x.shape
    ts = _tile(s, 512, 128)
    cos_t, sin_lo, sin_hi, half = tables
    one_roll = 2 * half == LANES

    def body(x_ref, c_ref, lo_ref, hi_ref, o_ref):
        c, s_lo, s_hi = c_ref[...], lo_ref[...], hi_ref[...]
        for j in range(w // LANES):
            cols = slice(j * LANES, (j + 1) * LANES)
            if odd_chunks_only and j % 2 == 0:
                o_ref[:, cols] = x_ref[:, cols].astype(o_ref.dtype)
                continue
            xv = x_ref[:, cols].astype(F32)
            if one_roll:
                y = xv * c + (pltpu.roll(xv * (s_lo + s_hi), half, 1) if transpose
                              else pltpu.roll(xv, half, 1) * (s_lo + s_hi))
            elif transpose:
                y = xv * c + pltpu.roll(xv * s_lo, LANES - half, 1) + pltpu.roll(xv * s_hi, half, 1)
            else:
                y = xv * c + pltpu.roll(xv, half, 1) * s_lo + pltpu.roll(xv, LANES - half, 1) * s_hi
            o_ref[:, cols] = y.astype(o_ref.dtype)

    row = pl.BlockSpec((ts, LANES), lambda i: (i, 0))
    return pl.pallas_call(
        body, name=name, grid=(s // ts,),
        in_specs=[pl.BlockSpec((ts, w), lambda i: (i, 0)), row, row, row],
        out_specs=pl.BlockSpec((ts, w), lambda i: (i, 0)),
        out_shape=jax.ShapeDtypeStruct((s, w), out_dtype),
        compiler_params=_params(("parallel",)))(x, cos_t, sin_lo, sin_hi)


def rope(x, tables, odd_chunks_only, out_dtype, tag):
    @jax.custom_vjp
    def op(x):
        return _rope_call(x, tables, odd_chunks_only, False, out_dtype, f"rope_{tag}")

    def fwd(x):
        return _rope_call(x, tables, odd_chunks_only, False, out_dtype, f"rope_{tag}"), None

    def bwd(_, dy):
        return (_rope_call(dy, tables, odd_chunks_only, True, x.dtype, f"rope_{tag}_bwd"),)

    op.defvjp(fwd, bwd)
    return op(x)


def rope_tables(seq, dim):
    half = dim // 2
    pos = jnp.arange(seq, dtype=F32)
    inv_freq = ROPE_THETA ** (-jnp.arange(0, dim, 2, dtype=F32) / dim)
    ang = pos[:, None] * inv_freq[None, :]
    cos, sin = jnp.cos(ang), jnp.sin(ang)
    zero = jnp.zeros_like(sin)
    rest = jnp.zeros((seq, LANES - dim), F32)
    return (jnp.concatenate([cos, cos, rest], axis=1), jnp.concatenate([zero, sin, rest], axis=1),
            jnp.concatenate([-sin, zero, rest], axis=1), half)


def _attn_blocks(s):
    return min(256, s), min(512, s)


def _sb_blocks(s):
    b = min(256, s)
    return b, b


def _score_mask(delta, mode):
    if mode != "dilated":
        return None, delta >= 0
    mult = jnp.zeros(delta.shape, F32)
    for window, dilation in DILATED_PAIRS:
        hit = delta <= window
        if dilation > 1:
            hit = hit & ((delta & (dilation - 1)) == 0)
        mult = mult + jnp.where(hit, 1.0, 0.0)
    bias = jnp.where(mult > 2.5, math.log(3.0), jnp.where(mult > 1.5, math.log(2.0), 0.0))
    return bias, (delta >= 0) & (mult > 0.5)


def _kv_range(qi, bq, bk, mode):
    hi = (qi * bq + bq - 1) // bk
    if mode == "dilated":
        lo = jnp.maximum(qi * bq - MAX_WINDOW, 0) // bk
        return lo, lo, hi
    return 0, (qi * bq + 1) // bk, hi


def _attn_fwd_call(q, k, v, cq, ck, heads, scale, mode, name):
    s = q.shape[0]
    dk, dv = q.shape[1] // heads, v.shape[1] // heads
    bq, bk = _attn_blocks(s)
    nq, nk = s // bq, s // bk
    fox = mode == "fox"

    def body(*refs):
        q_ref, k_ref, v_ref = refs[:3]
        o_ref, o32_ref, lse_ref = refs[-3:]
        qi = pl.program_id(1)
        qv = q_ref[...].astype(BF16)
        row_col = (lax.broadcasted_iota(jnp.int32, (bq, bk), 0) - lax.broadcasted_iota(jnp.int32, (bq, bk), 1))
        lo, full, hi = _kv_range(qi, bq, bk, mode)

        def step(masked, kj, carry):
            m, l, acc = carry
            rows = pl.ds(pl.multiple_of(kj * bk, bk), bk)
            kb = k_ref[rows, :].astype(BF16)
            vb = v_ref[rows, :].astype(BF16)
            sc = lax.dot_general(qv, kb, (((1,), (1,)), ((), ())), preferred_element_type=F32) * scale
            if fox:
                sc = sc + refs[3][...] - refs[4][kj]
            if masked:
                bias, mask = _score_mask(row_col + (qi * bq - kj * bk), mode)
                if bias is not None:
                    sc = sc + bias
                sc = jnp.where(mask, sc, NEG_INF)
            m_new = jnp.maximum(m, jnp.max(sc, axis=1, keepdims=True))
            alpha = jnp.exp(m - m_new)
            p = jnp.exp(sc - m_new)
            l = l * alpha + jnp.sum(p, axis=1, keepdims=True)
            acc = acc * alpha + jnp.dot(p.astype(BF16), vb, preferred_element_type=F32)
            return m_new, l, acc

        carry = (jnp.full((bq, 1), NEG_INF, F32), jnp.zeros((bq, 1), F32), jnp.zeros((bq, dv), F32))
        carry = lax.fori_loop(lo, full, functools.partial(step, False), carry)
        m, l, acc = lax.fori_loop(full, hi + 1, functools.partial(step, True), carry)
        out = acc / l
        o_ref[...] = out.astype(o_ref.dtype)
        o32_ref[...] = out
        lse_ref[...] = m + jnp.log(l)

    in_specs = [pl.BlockSpec((bq, dk), lambda h, i: (i, h)), pl.BlockSpec((s, dk), lambda h, i: (0, h)),
                pl.BlockSpec((s, dv), lambda h, i: (0, h))]
    args = [q, k, v]
    if fox:
        in_specs += [pl.BlockSpec((None, bq, 1), lambda h, i: (h, i, 0)),
                     pl.BlockSpec((None, nk, 1, bk), lambda h, i: (h, 0, 0, 0))]
        args += [cq, ck]
    return pl.pallas_call(
        body, name=name, grid=(heads, nq), in_specs=in_specs,
        out_specs=[pl.BlockSpec((bq, dv), lambda h, i: (i, h)), pl.BlockSpec((bq, dv), lambda h, i: (i, h)),
                   pl.BlockSpec((None, bq, 1), lambda h, i: (h, i, 0))],
        out_shape=[jax.ShapeDtypeStruct((s, heads * dv), BF16), jax.ShapeDtypeStruct((s, heads * dv), F32),
                   jax.ShapeDtypeStruct((heads, s, 1), F32)],
        compiler_params=_params(("parallel", "arbitrary")))(*args)


def _attn_bwd_call(q, k, v, cq, ck, o, lse, do, heads, scale, mode, name):
    s = q.shape[0]
    dk, dv = q.shape[1] // heads, v.shape[1] // heads
    bq, bk = _attn_blocks(s)
    nq, nk = s // bq, s // bk
    fox = mode == "fox"
    n_in = 8 if fox else 6

    def body(*refs):
        q_ref, k_ref, v_ref = refs[:3]
        o_ref, lse_ref, do_ref = refs[n_in - 3:n_in]
        outs = refs[n_in:]
        dq_ref, dk_ref, dv_ref = outs[:3]
        scratch = outs[5:] if fox else outs[3:]
        dk_acc, dv_acc = scratch[:2]
        qi = pl.program_id(1)

        @pl.when(qi == 0)
        def _():
            dk_acc[...] = jnp.zeros_like(dk_acc)
            dv_acc[...] = jnp.zeros_like(dv_acc)
            if fox:
                scratch[2][...] = jnp.zeros_like(scratch[2])

        qv = q_ref[...].astype(BF16)
        dov = do_ref[...].astype(BF16)
        delta_o = jnp.sum(dov.astype(F32) * o_ref[...], axis=1, keepdims=True)
        lse = lse_ref[...]
        row_col = (lax.broadcasted_iota(jnp.int32, (bq, bk), 0) - lax.broadcasted_iota(jnp.int32, (bq, bk), 1))
        lo, full, hi = _kv_range(qi, bq, bk, mode)

        def step(masked, kj, carry):
            dq, dcq = carry
            rows = pl.ds(pl.multiple_of(kj * bk, bk), bk)
            kb = k_ref[rows, :].astype(BF16)
            vb = v_ref[rows, :].astype(BF16)
            sc = lax.dot_general(qv, kb, (((1,), (1,)), ((), ())), preferred_element_type=F32) * scale
            if fox:
                sc = sc + refs[3][...] - refs[4][kj]
            if masked:
                bias, mask = _score_mask(row_col + (qi * bq - kj * bk), mode)
                if bias is not None:
                    sc = sc + bias
                sc = jnp.where(mask, sc, NEG_INF)
            p = jnp.exp(sc - lse)
            dv_acc[rows, :] += lax.dot_general(p.astype(BF16), dov, (((0,), (0,)), ((), ())),
                                               preferred_element_type=F32)
            dp = lax.dot_general(dov, vb, (((1,), (1,)), ((), ())), preferred_element_type=F32)
            ds = p * (dp - delta_o)
            dsb = ds.astype(BF16)
            dk_acc[rows, :] += lax.dot_general(dsb, qv, (((0,), (0,)), ((), ())), preferred_element_type=F32)
            if fox:
                scratch[2][kj] -= jnp.sum(ds, axis=0, keepdims=True)
                dcq = dcq + jnp.sum(ds, axis=1, keepdims=True)
            return dq + jnp.dot(dsb, kb, preferred_element_type=F32), dcq

        carry = (jnp.zeros((bq, dk), F32), jnp.zeros((bq, 1), F32))
        carry = lax.fori_loop(lo, full, functools.partial(step, False), carry)
        dq, dcq = lax.fori_loop(full, hi + 1, functools.partial(step, True), carry)
        dq_ref[...] = (dq * scale).astype(dq_ref.dtype)
        if fox:
            outs[4][...] = dcq

        @pl.when(qi == nq - 1)
        def _():
            dk_ref[...] = (dk_acc[...] * scale).astype(dk_ref.dtype)
            dv_ref[...] = dv_acc[...].astype(dv_ref.dtype)
            if fox:
                outs[3][...] = scratch[2][...]

    in_specs = [pl.BlockSpec((bq, dk), lambda h, i: (i, h)), pl.BlockSpec((s, dk), lambda h, i: (0, h)),
                pl.BlockSpec((s, dv), lambda h, i: (0, h))]
    args = [q, k, v]
    if fox:
        in_specs += [pl.BlockSpec((None, bq, 1), lambda h, i: (h, i, 0)),
                     pl.BlockSpec((None, nk, 1, bk), lambda h, i: (h, 0, 0, 0))]
        args += [cq, ck]
    in_specs += [pl.BlockSpec((bq, dv), lambda h, i: (i, h)), pl.BlockSpec((None, bq, 1), lambda h, i: (h, i, 0)),
                 pl.BlockSpec((bq, dv), lambda h, i: (i, h))]
    args += [o, lse, do]
    out_specs = [pl.BlockSpec((bq, dk), lambda h, i: (i, h)), pl.BlockSpec((s, dk), lambda h, i: (0, h)),
                 pl.BlockSpec((s, dv), lambda h, i: (0, h))]
    out_shape = [jax.ShapeDtypeStruct(q.shape, q.dtype), jax.ShapeDtypeStruct(k.shape, k.dtype),
                 jax.ShapeDtypeStruct(v.shape, v.dtype)]
    scratch_shapes = [pltpu.VMEM((s, dk), F32), pltpu.VMEM((s, dv), F32)]
    if fox:
        out_specs.append(pl.BlockSpec((None, nk, 1, bk), lambda h, i: (h, 0, 0, 0)))
        out_shape.append(jax.ShapeDtypeStruct((heads, nk, 1, bk), F32))
        out_specs.append(pl.BlockSpec((None, bq, 1), lambda h, i: (h, i, 0)))
        out_shape.append(jax.ShapeDtypeStruct((heads, s, 1), F32))
        scratch_shapes.append(pltpu.VMEM((nk, 1, bk), F32))
    return pl.pallas_call(
        body, name=name, grid=(heads, nq), in_specs=in_specs, out_specs=out_specs, out_shape=out_shape,
        scratch_shapes=scratch_shapes, compiler_params=_params(("arbitrary", "arbitrary")))(*args)


def attention(q, k, v, cum, heads, scale, mode, tag):
    s = q.shape[0]
    _, bk = _attn_blocks(s)

    def layouts(cum):
        return cum.reshape(heads, s, 1), cum.reshape(heads, s // bk, 1, bk)

    @jax.custom_vjp
    def op(q, k, v, cum):
        cq, ck = layouts(cum)
        return _attn_fwd_call(q, k, v, cq, ck, heads, scale, mode, f"attn_{tag}")[0]

    def fwd(q, k, v, cum):
        cq, ck = layouts(cum)
        o, o32, lse = _attn_fwd_call(q, k, v, cq, ck, heads, scale, mode, f"attn_{tag}")
        return o, (q, k, v, cum, o32, lse)

    def bwd(saved, do):
        q, k, v, cum, o, lse = saved
        cq, ck = layouts(cum)
        res = _attn_bwd_call(q, k, v, cq, ck, o, lse, do, heads, scale, mode, f"attn_{tag}_bwd")
        dcum = res[3].reshape(heads, s) + res[4].reshape(heads, s) if mode == "fox" else jnp.zeros_like(cum)
        return res[0], res[1], res[2], dcum

    op.defvjp(fwd, bwd)
    return op(q, k, v, cum)


def _log_sigmoid(z):
    return jnp.minimum(z, 0.0) - jnp.log(1.0 + jnp.exp(-jnp.abs(z)))


def _tri_dot(x, tri):
    hi = x.astype(BF16)
    lo = (x - hi.astype(F32)).astype(BF16)
    return jnp.dot(hi, tri, preferred_element_type=F32) + jnp.dot(lo, tri, preferred_element_type=F32)


def _sb_call(q, k, v, keep_all, do, heads, scale, backward, name):
    s = q.shape[0]
    d = q.shape[1] // heads
    bq, bk = _sb_blocks(s)
    nq = s // bq

    def body(*refs):
        q_ref, k_ref, v_ref = refs[:3]
        qi = pl.program_id(1)
        qv = q_ref[...].astype(BF16)
        q_pos = qi * bq + lax.broadcasted_iota(jnp.int32, (bq, bk), 0)
        k_col = lax.broadcasted_iota(jnp.int32, (bq, bk), 1)
        r_i = lax.broadcasted_iota(jnp.int32, (bk, bk), 0)
        c_i = lax.broadcasted_iota(jnp.int32, (bk, bk), 1)
        hi = (qi * bq + bq - 1) // bk

        def logits(kj):
            rows = pl.ds(pl.multiple_of(kj * bk, bk), bk)
            kb = k_ref[rows, :].astype(BF16)
            z = lax.dot_general(qv, kb, (((1,), (1,)), ((), ())), preferred_element_type=F32) * scale
            past = (kj * bk + k_col) < q_pos
            ls = _log_sigmoid(z)
            lk = jnp.where(past, ls - z, 0.0)
            return rows, kb, z, past, ls, lk

        if not backward:
            o_ref, keep_ref = refs[3:5]
            after = jnp.where(r_i > c_i, 1.0, 0.0).astype(BF16)

            def step(it, carry):
                keep_right, acc = carry
                rows, _, _, past, ls, lk = logits(hi - it)
                a = jnp.where(past, jnp.exp(ls + _tri_dot(lk, after) + keep_right), 0.0)
                acc = acc + jnp.dot(a.astype(BF16), v_ref[rows, :].astype(BF16), preferred_element_type=F32)
                return keep_right + jnp.sum(lk, axis=1, keepdims=True), acc

            keep, acc = lax.fori_loop(0, hi + 1, step, (jnp.zeros((bq, 1), F32), jnp.zeros((bq, d), F32)))
            o_ref[...] = acc.astype(o_ref.dtype)
            keep_ref[...] = keep
            return

        keep_ref, do_ref, dq_ref, dk_ref, dv_ref, dk_acc, dv_acc = refs[3:]

        @pl.when(qi == 0)
        def _():
            dk_acc[...] = jnp.zeros_like(dk_acc)
            dv_acc[...] = jnp.zeros_like(dv_acc)

        dov = do_ref[...].astype(BF16)
        keep_total = keep_ref[...]
        upto = jnp.where(r_i <= c_i, 1.0, 0.0).astype(BF16)
        before = jnp.where(r_i < c_i, 1.0, 0.0).astype(BF16)

        def step(kj, carry):
            keep_left, g_left, dq = carry
            rows, kb, z, past, ls, lk = logits(kj)
            between = keep_total - (keep_left + _tri_dot(lk, upto))
            a = jnp.where(past, jnp.exp(ls + between), 0.0)
            vb = v_ref[rows, :].astype(BF16)
            da = lax.dot_general(dov, vb, (((1,), (1,)), ((), ())), preferred_element_type=F32)
            g = a * da
            g_before = g_left + _tri_dot(g, before)
            dz = jnp.where(past, g * jnp.exp(ls - z) - jnp.exp(ls) * g_before, 0.0)
            dzb = dz.astype(BF16)
            dk_acc[rows, :] += lax.dot_general(dzb, qv, (((0,), (0,)), ((), ())), preferred_element_type=F32)
            dv_acc[rows, :] += lax.dot_general(a.astype(BF16), dov, (((0,), (0,)), ((), ())),
                                               preferred_element_type=F32)
            dq = dq + jnp.dot(dzb, kb, preferred_element_type=F32)
            return (keep_left + jnp.sum(lk, axis=1, keepdims=True),
                    g_left + jnp.sum(g, axis=1, keepdims=True), dq)

        init = (jnp.zeros((bq, 1), F32), jnp.zeros((bq, 1), F32), jnp.zeros((bq, d), F32))
        _, _, dq = lax.fori_loop(0, hi + 1, step, init)
        dq_ref[...] = (dq * scale).astype(dq_ref.dtype)

        @pl.when(qi == nq - 1)
        def _():
            dk_ref[...] = (dk_acc[...] * scale).astype(dk_ref.dtype)
            dv_ref[...] = dv_acc[...].astype(dv_ref.dtype)

    blk_q = pl.BlockSpec((bq, d), lambda h, i: (i, h))
    blk_kv = pl.BlockSpec((s, d), lambda h, i: (0, h))
    blk_row = pl.BlockSpec((None, bq, 1), lambda h, i: (h, i, 0))
    if not backward:
        return pl.pallas_call(
            body, name=name, grid=(heads, nq), in_specs=[blk_q, blk_kv, blk_kv], out_specs=[blk_q, blk_row],
            out_shape=[jax.ShapeDtypeStruct(q.shape, BF16), jax.ShapeDtypeStruct((heads, s, 1), F32)],
            compiler_params=_params(("parallel", "arbitrary")))(q, k, v)
    return pl.pallas_call(
        body, name=name, grid=(heads, nq), in_specs=[blk_q, blk_kv, blk_kv, blk_row, blk_q],
        out_specs=[blk_q, blk_kv, blk_kv],
        out_shape=[jax.ShapeDtypeStruct(q.shape, q.dtype), jax.ShapeDtypeStruct(k.shape, k.dtype),
                   jax.ShapeDtypeStruct(v.shape, v.dtype)],
        scratch_shapes=[pltpu.VMEM((s, d), F32), pltpu.VMEM((s, d), F32)],
        compiler_params=_params(("arbitrary", "arbitrary")))(q, k, v, keep_all, do)


def stick_breaking(q, k, v, heads, scale, tag):
    @jax.custom_vjp
    def op(q, k, v):
        return _sb_call(q, k, v, None, None, heads, scale, False, f"sb_{tag}")[0]

    def fwd(q, k, v):
        o, keep_all = _sb_call(q, k, v, None, None, heads, scale, False, f"sb_{tag}")
        return o, (q, k, v, keep_all)

    def bwd(saved, do):
        q, k, v, keep_all = saved
        return tuple(_sb_call(q, k, v, keep_all, do, heads, scale, True, f"sb_{tag}_bwd"))

    op.defvjp(fwd, bwd)
    return op(q, k, v)


def _tri_dot3(x, tri):
    hi = x.astype(BF16)
    r1 = x - hi.astype(F32)
    mid = r1.astype(BF16)
    lo = (r1 - mid.astype(F32)).astype(BF16)
    return (jnp.dot(hi, tri, preferred_element_type=F32) + jnp.dot(mid, tri, preferred_element_type=F32)
            + jnp.dot(lo, tri, preferred_element_type=F32))


def _gate_call(f, bias_b, dcum, name):
    heads, r, _ = f.shape
    backward = dcum is not None

    def body(*refs):
        f_ref, b_ref = refs[:2]
        r_i = lax.broadcasted_iota(jnp.int32, (LANES, LANES), 0)
        c_i = lax.broadcasted_iota(jnp.int32, (LANES, LANES), 1)
        x = f_ref[...] + b_ref[...]
        if not backward:
            o_ref, ls_ref = refs[2:]
            ls_ref[...] = _log_sigmoid(x)
            o_ref[...] = _tri_dot3(ls_ref[...], jnp.where(r_i <= c_i, 1.0, 0.0).astype(BF16))

            def row(i, carry):
                o_ref[pl.ds(i, 1), :] = o_ref[pl.ds(i, 1), :] + carry
                return carry + jnp.sum(ls_ref[pl.ds(i, 1), :], axis=1, keepdims=True)

            lax.fori_loop(0, r, row, jnp.zeros((1, 1), F32))
            return

        dc_ref, df_ref, db_ref, acc_ref = refs[2:]
        acc_ref[...] = _tri_dot3(dc_ref[...], jnp.where(r_i >= c_i, 1.0, 0.0).astype(BF16))

        def row(it, carry):
            i = r - 1 - it
            acc_ref[pl.ds(i, 1), :] = acc_ref[pl.ds(i, 1), :] + carry
            return carry + jnp.sum(dc_ref[pl.ds(i, 1), :], axis=1, keepdims=True)

        lax.fori_loop(0, r, row, jnp.zeros((1, 1), F32))
        df = acc_ref[...] * jnp.exp(_log_sigmoid(-x))
        df_ref[...] = df
        lane = lax.broadcasted_iota(jnp.int32, (1, LANES), 1)
        db_ref[...] = jnp.where(lane == 0, jnp.sum(df), 0.0)

    blk = pl.BlockSpec((None, r, LANES), lambda h: (h, 0, 0))
    blk_b = pl.BlockSpec((None, 1, LANES), lambda h: (h, 0, 0))
    if not backward:
        return pl.pallas_call(
            body, name=name, grid=(heads,), in_specs=[blk, blk_b], out_specs=blk,
            out_shape=jax.ShapeDtypeStruct(f.shape, F32), scratch_shapes=[pltpu.VMEM((r, LANES), F32)],
            compiler_params=_params(("parallel",)))(f, bias_b)
    return pl.pallas_call(
        body, name=name, grid=(heads,), in_specs=[blk, blk_b, blk], out_specs=[blk, blk_b],
        out_shape=[jax.ShapeDtypeStruct(f.shape, F32), jax.ShapeDtypeStruct(bias_b.shape, F32)],
        scratch_shapes=[pltpu.VMEM((r, LANES), F32)],
        compiler_params=_params(("parallel",)))(f, bias_b, dcum)


def forget_gate_cumsum(f, bias_b, tag):
    @jax.custom_vjp
    def op(f, bias_b):
        return _gate_call(f, bias_b, None, f"gate_{tag}")

    def fwd(f, bias_b):
        return _gate_call(f, bias_b, None, f"gate_{tag}"), (f, bias_b)

    def bwd(saved, dcum):
        return tuple(_gate_call(saved[0], saved[1], dcum, f"gate_{tag}_bwd"))

    op.defvjp(fwd, bwd)
    return op(f, bias_b)


def _swiglu_call(g, u, dact, name):
    s, n = g.shape
    ts = _tile(s, 256, 128)
    backward = dact is not None

    def body(*refs):
        gv = refs[0][...].astype(F32)
        uv = refs[1][...].astype(F32)
        sig = 1.0 / (1.0 + jnp.exp(-gv))
        if not backward:
            refs[2][...] = (gv * sig * uv).astype(refs[2].dtype)
            return
        dv = refs[2][...].astype(F32)
        refs[3][...] = (dv * uv * sig * (1.0 + gv * (1.0 - sig))).astype(refs[3].dtype)
        refs[4][...] = (dv * gv * sig).astype(refs[4].dtype)

    blk = pl.BlockSpec((ts, n), lambda i: (i, 0))
    shape = jax.ShapeDtypeStruct((s, n), g.dtype)
    if not backward:
        return pl.pallas_call(body, name=name, grid=(s // ts,), in_specs=[blk, blk], out_specs=blk,
                              out_shape=shape, compiler_params=_params(("parallel",)))(g, u)
    return pl.pallas_call(body, name=name, grid=(s // ts,), in_specs=[blk, blk, blk], out_specs=[blk, blk],
                          out_shape=[shape, shape], compiler_params=_params(("parallel",)))(g, u, dact)


def swiglu(g, u, tag):
    @jax.custom_vjp
    def op(g, u):
        return _swiglu_call(g, u, None, f"swiglu_{tag}")

    def fwd(g, u):
        return _swiglu_call(g, u, None, f"swiglu_{tag}"), (g, u)

    def bwd(saved, dact):
        return tuple(_swiglu_call(saved[0], saved[1], dact, f"swiglu_{tag}_bwd"))

    op.defvjp(fwd, bwd)
    return op(g, u)


def final_norm_loss(x, gain, target):
    s, d = x.shape
    ts = _tile(s, 256, 128)

    def body(x_ref, g_ref, t_ref, sq_ref, dx_ref, dg_ref):
        @pl.when(pl.program_id(0) == 0)
        def _():
            sq_ref[...] = jnp.zeros_like(sq_ref)
            dg_ref[...] = jnp.zeros_like(dg_ref)

        xv = x_ref[...]
        r = lax.rsqrt(jnp.mean(xv * xv, axis=1, keepdims=True) + EPS)
        xhat = xv * r
        err = xhat * g_ref[...] - t_ref[...]
        sq_ref[...] += jnp.sum(err * err)
        dy = err * (1.0 / d)
        gdy = dy * g_ref[...]
        dx_ref[...] = r * (gdy - xhat * jnp.mean(gdy * xhat, axis=1, keepdims=True))
        dg_ref[...] += jnp.sum(dy * xhat, axis=0, keepdims=True)

    blk = pl.BlockSpec((ts, d), lambda i: (i, 0))
    row = pl.BlockSpec((1, d), lambda i: (0, 0))
    return pl.pallas_call(
        body, name="final_norm_loss", grid=(s // ts,), in_specs=[blk, row, blk],
        out_specs=[pl.BlockSpec((1, LANES), lambda i: (0, 0)), blk, row],
        out_shape=[jax.ShapeDtypeStruct((1, LANES), F32), jax.ShapeDtypeStruct((s, d), F32),
                   jax.ShapeDtypeStruct((1, d), F32)],
        compiler_params=_params(("arbitrary",)))(x, gain.reshape(1, d), target)


def adamw(w, g, m, v, name):
    rows, cols = w.shape
    tr = _tile(rows, 256, 8)
    c1 = 1.0 - ADAM_B1 ** ADAM_STEP
    c2 = 1.0 - ADAM_B2 ** ADAM_STEP

    def body(w_ref, g_ref, m_ref, v_ref, d_ref, nm_ref, nv_ref):
        gv = g_ref[...]
        m_new = ADAM_B1 * m_ref[...] + (1.0 - ADAM_B1) * gv
        v_new = ADAM_B2 * v_ref[...] + (1.0 - ADAM_B2) * (gv * gv)
        d_ref[...] = -ADAM_LR * ((m_new / c1) / (jnp.sqrt(v_new / c2) + ADAM_EPS) + ADAM_WD * w_ref[...])
        nm_ref[...] = m_new
        nv_ref[...] = v_new

    blk = pl.BlockSpec((tr, cols), lambda i: (i, 0))
    shape = jax.ShapeDtypeStruct((rows, cols), F32)
    return pl.pallas_call(body, name=name, grid=(rows // tr,), in_specs=[blk] * 4, out_specs=[blk] * 3,
                          out_shape=[shape] * 3, compiler_params=_params(("parallel",)))(w, g, m, v)


ANY = pl.BlockSpec(memory_space=pl.ANY)


def _position():
    return lax.axis_index("x"), lax.axis_index("y"), lax.axis_index("c")


HBM = pl.BlockSpec(memory_space=pltpu.HBM)
SEM = pl.BlockSpec(memory_space=pltpu.SEMAPHORE)
DATAFLOW = pltpu.SideEffectType.DATAFLOW_SIDE_EFFECTING


def _gather_copies(srcs, lands, send_sems, recv_sems):
    x, y, c = _position()
    chips = [(1 - x, y), (x, 1 - y), (1 - x, 1 - y)]
    return [pltpu.make_async_remote_copy(
        src_ref=srcs[a], dst_ref=lands[a].at[2 * x + y], send_sem=send_sems.at[3 * a + k], recv_sem=recv_sems.at[3 * a + k],
        device_id=(chips[k][0], chips[k][1], c), device_id_type=MESH) for a in range(len(srcs)) for k in range(3)]


def _scatter_copies(srcs, lands, send_sems, recv_sems):
    x, y, c = _position()
    out = []
    for a in range(len(srcs)):
        for k in range(1, 8):
            px, py, pc = (1 - x if k & 4 else x, 1 - y if k & 2 else y, 1 - c if k & 1 else c)
            out.append(pltpu.make_async_remote_copy(
                src_ref=srcs[a].at[2 * px + py, pc], dst_ref=lands[a].at[4 * x + 2 * y + c],
                send_sem=send_sems.at[7 * a + k - 1], recv_sem=recv_sems.at[7 * a + k - 1],
                device_id=(px, py, pc), device_id_type=MESH))
    return out


def _swap_copies(srcs, lands, send_sems, recv_sems):
    x, y, c = _position()
    return [pltpu.make_async_remote_copy(
        src_ref=srcs[a], dst_ref=lands[a].at[c], send_sem=send_sems.at[a], recv_sem=recv_sems.at[a],
        device_id=(x, y, 1 - c), device_id_type=MESH) for a in range(len(srcs))]


EXCHANGES = {"gather": (_gather_copies, 3), "scatter": (_scatter_copies, 7), "swap": (_swap_copies, 1)}


def start_exchange(kind, srcs, lands, after, name):
    copies, per_array = EXCHANGES[kind]
    n = len(srcs)

    def body(*refs):
        for cp in copies(refs[:n], refs[n:2 * n], refs[2 * n + 1], refs[2 * n + 2]):
            cp.start()
        refs[-1][...] = jnp.zeros_like(refs[-1])

    sems = pltpu.SemaphoreType.DMA((n * per_array,))
    hbm = [pltpu.HBM(a.shape, a.dtype) for a in list(srcs) + list(lands)]
    outs = pl.pallas_call(
        body, name=name, in_specs=[HBM] * (2 * n) + [ANY],
        out_shape=(sems, sems, *hbm, jax.ShapeDtypeStruct((8, LANES), F32)),
        out_specs=(SEM, SEM, *[HBM] * (2 * n), pl.BlockSpec(memory_space=pltpu.VMEM)),
        input_output_aliases={i: 2 + i for i in range(2 * n)},
        compiler_params=pltpu.CompilerParams(has_side_effects=DATAFLOW),
    )(*[pltpu.with_memory_space_constraint(a, pltpu.HBM) for a in list(srcs) + list(lands)], after)
    return (kind, n, outs[:-1]), outs[-1]


def finish_exchange(state, after, name):
    kind, n, (send_sems, recv_sems, *buffers) = state
    copies, _ = EXCHANGES[kind]

    def body(*refs):
        for cp in copies(refs[:n], refs[n:2 * n], refs[2 * n], refs[2 * n + 1]):
            cp.wait_send()
            cp.wait_recv()

    outs = pl.pallas_call(
        body, name=name, in_specs=[HBM] * (2 * n) + [SEM, SEM, ANY],
        out_shape=[pltpu.HBM(a.shape, a.dtype) for a in buffers], out_specs=[HBM] * (2 * n),
        input_output_aliases={i: i for i in range(2 * n)},
        compiler_params=pltpu.CompilerParams(has_side_effects=DATAFLOW),
    )(*buffers, send_sems, recv_sems, after)
    return outs[n:]


def _own_slot(land_shape, dtype, block, slot):
    start = (slot,) + (0,) * block.ndim
    return lax.dynamic_update_slice(lax.empty(land_shape, dtype), block[None], start)


def broadcast_all(block, name):
    r, lanes = block.shape

    def body(b_ref, out_ref, send_sems, recv_sems, local_sem):
        x, y, c = _position()
        me = 4 * x + 2 * y + c
        mine = pltpu.make_async_copy(b_ref, out_ref.at[me], local_sem)
        mine.start()

        def copy(k, to_me):
            px, py, pc = (1 - x if k & 4 else x, 1 - y if k & 2 else y, 1 - c if k & 1 else c)
            slot = 4 * px + 2 * py + pc if to_me else me
            return pltpu.make_async_remote_copy(
                src_ref=b_ref, dst_ref=out_ref.at[slot], send_sem=send_sems.at[k - 1],
                recv_sem=recv_sems.at[k - 1], device_id=(px, py, pc), device_id_type=MESH)

        sends = [copy(k, False) for k in range(1, 8)]
        for cp in sends:
            cp.start()
        for k in range(1, 8):
            copy(k, True).wait_recv()
        for cp in sends:
            cp.wait_send()
        mine.wait()

    return pl.pallas_call(
        body, name=name, in_specs=[ANY], out_specs=ANY,
        out_shape=jax.ShapeDtypeStruct((8, r, lanes), block.dtype),
        scratch_shapes=[pltpu.SemaphoreType.DMA((7,)), pltpu.SemaphoreType.DMA((7,)), pltpu.SemaphoreType.DMA])(block)


def sum_slots(stack, name):
    n, r, lanes = stack.shape
    tr = _tile(r, max(16, (1 << 22) // (n * lanes * stack.dtype.itemsize) // 16 * 16), 16)

    def body(s_ref, o_ref):
        acc = s_ref[0].astype(F32)
        for i in range(1, n):
            acc = acc + s_ref[i].astype(F32)
        o_ref[...] = acc

    return pl.pallas_call(
        body, name=name, grid=(r // tr,), in_specs=[pl.BlockSpec((n, tr, lanes), lambda i: (0, i, 0))],
        out_specs=pl.BlockSpec((tr, lanes), lambda i: (i, 0)), out_shape=jax.ShapeDtypeStruct((r, lanes), F32),
        compiler_params=_params(("parallel",)))(stack)


def _pack(arrays, dtype, row_align):
    flat = jnp.concatenate([a.reshape(-1).astype(dtype) for a in arrays])
    rows = -(-flat.shape[0] // LANES)
    rows = -(-rows // row_align) * row_align
    return jnp.pad(flat, (0, rows * LANES - flat.shape[0])).reshape(rows, LANES)


def _unpack(flat, shapes):
    flat = flat.reshape(flat.shape[:-2] + (-1,))
    out, at = [], 0
    for shp in shapes:
        n = math.prod(shp)
        out.append(flat[..., at:at + n].reshape(flat.shape[:-1] + tuple(shp)))
        at += n
    return out


def _pad_cols(w):
    return jnp.pad(w, ((0, 0), (0, -w.shape[1] % LANES)))


def split_columns(raw, width, widths, dtypes):
    wp = raw.shape[1] * width // sum(widths)
    jn = raw.shape[1] // wp

    def cut(raw):
        whole = jnp.concatenate([raw[:, j * wp:j * wp + width] for j in range(jn)], axis=1)
        out, at = [], 0
        for n, dt in zip(widths, dtypes):
            out.append(whole[:, at:at + n].astype(dt))
            at += n
        return tuple(out)

    @jax.custom_vjp
    def op(raw):
        return cut(raw)

    def fwd(raw):
        return cut(raw), None

    def bwd(_, cts):
        whole = jnp.concatenate([ct.astype(BF16) for ct in cts], axis=1)
        gap = jnp.zeros((whole.shape[0], wp - width), BF16)
        blocks = []
        for j in range(jn):
            blocks += [whole[:, j * width:(j + 1) * width], gap]
        return (jnp.concatenate(blocks, axis=1).astype(raw.dtype),)

    op.defvjp(fwd, bwd)
    return op(raw)


def _layer(x, mats, smalls, tables, widths, heads, tag):
    s, d_model = x.shape
    gw = heads * HEAD_DIM
    w_in, w_uq, w_ukv, w_out, w_gate, w_up, w_down = mats
    attn_norm, q_norm, kv_norm, f_bias, group_norm, ffn_norm = smalls
    rope_full, rope_mla = tables
    no_cum = jnp.zeros((heads, s), F32)

    h = rms_norm(x, attn_norm, 1, BF16, f"attn_{tag}")
    raw = matmul(h, w_in, F32, f"in_{tag}")
    sizes = [Q_LORA, KV_LORA, QK_ROPE, 2 * gw, gw, gw, gw, gw, heads, gw, gw, gw]
    kinds = [F32, F32, F32, F32, BF16, BF16, BF16, BF16, F32, BF16, BF16, BF16]
    q_lat, kv_lat, k_rope, qk_b, v_b, q_c, k_c, v_c, f_logit, q_d, k_d, v_d = split_columns(
        raw, widths[0], sizes, kinds)

    q_all = matmul(rms_norm(q_lat, q_norm, 1, BF16, f"qlat_{tag}"), w_uq, BF16, f"uq_{tag}")
    kv_all = matmul(rms_norm(kv_lat, kv_norm, 1, BF16, f"kvlat_{tag}"), w_ukv, BF16, f"ukv_{tag}")
    q_a = rope(q_all, rope_mla, True, BF16, f"qpe_{tag}")
    k_pe = rope(_pad_cols(k_rope), rope_mla, False, BF16, f"kpe_{tag}")
    kv4 = kv_all.reshape(s, heads, 2, LANES)
    k_a = jnp.stack([kv4[:, :, 0], jnp.broadcast_to(k_pe[:, None, :], (s, heads, LANES))], axis=2).reshape(s, 2 * gw)
    out_a = attention(q_a, k_a, kv4[:, :, 1].reshape(s, gw), no_cum, heads, (QK_NOPE + QK_ROPE) ** -0.5,
                      "causal", f"a_{tag}")

    qk_b = rope(qk_b, rope_full, False, BF16, f"qkb_{tag}")
    out_b = attention(qk_b[:, :gw], qk_b[:, gw:], v_b, no_cum, heads, HEAD_DIM ** -0.5, "dilated", f"b_{tag}")

    f_rows = f_logit.T.reshape(heads, s // LANES, LANES)
    bias_b = jnp.broadcast_to(f_bias[:, None, None], (heads, 1, LANES))
    cum = forget_gate_cumsum(f_rows, bias_b, tag).reshape(heads, s)
    out_c = attention(q_c, k_c, v_c, cum, heads, HEAD_DIM ** -0.5, "fox", f"c_{tag}")

    out_d = stick_breaking(q_d, k_d, v_d, heads, HEAD_DIM ** -0.5, f"d_{tag}")

    groups = jnp.concatenate([out_a, out_b, out_c, out_d], axis=1)
    x = matmul_res(rms_norm(groups, group_norm, 4, BF16, f"group_{tag}"), w_out.reshape(-1, d_model), x, f"out_{tag}")

    h2 = rms_norm(x, ffn_norm, 1, BF16, f"ffn_{tag}")
    act = swiglu(matmul(h2, w_gate, BF16, f"gate_{tag}"), matmul(h2, w_up, BF16, f"up_{tag}"), tag)
    return matmul_res(act, w_down.reshape(-1, d_model), x, f"down_{tag}")


def kernel(x, attn_norm, w_in, mla_q_norm, w_uq, mla_kv_norm, w_ukv, fox_forget_bias, group_norm, w_out, ffn_norm, w_gate, w_up, w_down, final_norm, loss_target, m_attn_norm, m_w_in, m_mla_q_norm, m_w_uq, m_mla_kv_norm, m_w_ukv, m_fox_forget_bias, m_group_norm, m_w_out, m_ffn_norm, m_w_gate, m_w_up, m_w_down, m_final_norm, v_attn_norm, v_w_in, v_mla_q_norm, v_w_uq, v_mla_kv_norm, v_w_ukv, v_fox_forget_bias, v_group_norm, v_w_out, v_ffn_norm, v_w_gate, v_w_up, v_w_down, v_final_norm):
    depth = w_in.shape[0]
    _, s, d_model = x.shape
    heads = d_model // 4 // HEAD_DIM
    big = [w_in, w_uq, w_ukv, w_out, w_gate, w_up, w_down]
    big_m = [m_w_in, m_w_uq, m_w_ukv, m_w_out, m_w_gate, m_w_up, m_w_down]
    big_v = [v_w_in, v_w_uq, v_w_ukv, v_w_out, v_w_gate, v_w_up, v_w_down]
    small = [attn_norm, mla_q_norm, mla_kv_norm, fox_forget_bias, group_norm, ffn_norm]
    small_m = [m_attn_norm, m_mla_q_norm, m_mla_kv_norm, m_fox_forget_bias, m_group_norm, m_ffn_norm, m_final_norm]
    small_v = [v_attn_norm, v_mla_q_norm, v_mla_kv_norm, v_fox_forget_bias, v_group_norm, v_ffn_norm, v_final_norm]
    widths = [w.shape[2] for w in big]
    tables = (rope_tables(s, HEAD_DIM), rope_tables(s, QK_ROPE))

    px, py, pc = _position()

    def start_gather(l, after):
        shards = [_pad_cols(w[l]).astype(BF16) for w in big]
        lands = [_own_slot((4,) + sh.shape, BF16, sh, 2 * px + py) for sh in shards]
        return start_exchange("gather", shards, lands, after, f"gather_start_{l}")

    state, token = start_gather(0, x)
    gathered = finish_exchange(state, token, "gather_wait_0")
    x_l, vjps = x[0], []
    for l in range(depth):
        small_l = [p[l] for p in small]
        if l + 1 < depth:
            state, token = start_gather(l + 1, gathered[1])
            small_l[0] = small_l[0] + token[0, 0]
        x_l, vjp_l = jax.vjp(lambda g, sm, xx, l=l: _layer(xx, g, sm, tables, widths, heads, str(l)),
                             gathered, small_l, x_l)
        vjps.append(vjp_l)
        if l + 1 < depth:
            gathered = finish_exchange(state, x_l, f"gather_wait_{l + 1}")

    sq, dx, d_final = final_norm_loss(x_l, final_norm, loss_target[0])
    loss = 0.5 / d_model * lax.psum(sq[0, 0], ("x", "y", "c"))

    d_small = [None] * depth
    big_grads = [None] * depth
    scatters, swaps = {}, {}

    def finish_scatter(l, after):
        recv = finish_exchange(scatters.pop(l), after, f"scatter_wait_{l}")
        halves = [sum_slots(r, f"sum_{l}_{i}") for i, r in enumerate(recv)]
        lands = [_own_slot((2,) + h.shape, F32, h, pc) for h in halves]
        swaps[l], token = start_exchange("swap", halves, lands, after, f"swap_start_{l}")
        return token

    def finish_swap(l, after):
        full = finish_exchange(swaps.pop(l), after, f"swap_wait_{l}")
        big_grads[l] = [f.reshape(-1, f.shape[2])[:, :wd] for f, wd in zip(full, widths)]

    for l in reversed(range(depth)):
        d_g, d_small[l], dx = vjps[l](dx)
        parts = [g.reshape(4, 2, g.shape[1] // 2, g.shape[2]) for g in d_g]
        lands = [_own_slot((8,) + p.shape[2:], BF16, p[2 * px + py, pc], 4 * px + 2 * py + pc) for p in parts]
        scatters[l], token = start_exchange("scatter", parts, lands, dx, f"scatter_start_{l}")
        tokens = token[0, 0]
        if l + 2 < depth:
            finish_swap(l + 2, dx)
        if l + 1 < depth:
            tokens = tokens + finish_scatter(l + 1, dx)[0, 0]
        dx = dx + tokens
    if depth > 1:
        finish_swap(1, dx)
    token = finish_scatter(0, dx)
    finish_swap(0, token)
    big_g = [jnp.stack([big_grads[l][i] for l in range(depth)]) for i in range(len(big))]
    d_small = [jnp.stack([d_small[l][i] for l in range(depth)]) for i in range(len(small))]

    small_all = small + [final_norm]
    packed = _pack(d_small + [d_final], F32, 8)
    small_sum = sum_slots(broadcast_all(packed, "small_gather"), "small_sum")
    small_g = _unpack(small_sum, [p.shape for p in small_all])

    def update(w, g, m, v, name):
        shp = w.shape
        two_d = (-1, shp[-1])
        return [o.reshape(shp) for o in adamw(w.reshape(two_d), g.reshape(two_d), m.reshape(two_d),
                                              v.reshape(two_d), name)]

    big_u = [update(w, g, m, v, f"adamw_big{i}") for i, (w, g, m, v) in enumerate(zip(big, big_g, big_m, big_v))]
    small_u = adamw(_pack(small_all, F32, 8), small_sum, _pack(small_m, F32, 8), _pack(small_v, F32, 8), "adamw_small")
    small_u = [_unpack(u, [p.shape for p in small_all]) for u in small_u]

    def ordered(bigs, smalls):
        a_n, q_n, kv_n, f_b, g_n, f_n, fin = smalls
        wi, uq, ukv, wo, wg, wu, wd = bigs
        return [a_n, wi, q_n, uq, kv_n, ukv, f_b, g_n, wo, f_n, wg, wu, wd, fin]

    grads = ordered(big_g, small_g)
    deltas = ordered([u[0] for u in big_u], small_u[0])
    new_m = ordered([u[1] for u in big_u], small_u[1])
    new_v = ordered([u[2] for u in big_u], small_u[2])
    return (loss, dx[None], *grads, *deltas, *new_m, *new_v)
```

```python
import functools
import math

import jax
import jax.numpy as jnp
from jax import lax
from jax.experimental import pallas as pl
from jax.experimental.pallas import tpu as pltpu

F32 = jnp.float32
BF16 = jnp.bfloat16
MESH = pl.DeviceIdType.MESH

HEAD_DIM = 128
Q_LORA = 512
KV_LORA = 512
QK_NOPE = 128
QK_ROPE = 64
DILATED_PAIRS = ((128, 1), (512, 4), (2048, 16))
MAX_WINDOW = max(w for w, _ in DILATED_PAIRS)
ROPE_THETA = 10000.0
EPS = 1e-6
NEG_INF = -1e30
ADAM_LR = 0.001
ADAM_B1 = 0.9
ADAM_B2 = 0.999
ADAM_EPS = 1e-08
ADAM_WD = 0.01
ADAM_STEP = 10

LANES = 128
VMEM_LIMIT_BYTES = 56 * 1024 * 1024


def _params(semantics=None):
    return pltpu.CompilerParams(dimension_semantics=semantics, vmem_limit_bytes=VMEM_LIMIT_BYTES)


MATMUL_VMEM_BUDGET = 40 * 1024 * 1024
MAX_TILE_COLS = 1536


def _fit_tile(n, cap, estimate):
    t = _tile(n, cap, LANES)
    while estimate(t) > MATMUL_VMEM_BUDGET and t > LANES:
        t = _tile(n, t - LANES, LANES)
    return t


def _tile(n, cap, align):
    if n <= cap:
        return n
    t = cap - cap % align
    while t >= align:
        if n % t == 0:
            return t
        t -= align
    raise ValueError(f"no tile for {n} under {cap}")


def _col_blocks(b):
    return (b.shape[0], b.shape[2]) if b.ndim == 3 else (1, b.shape[1])


def _mm_nn(a, b, res, out_dtype, name):
    m, k = a.shape
    jn, np_ = _col_blocks(b)
    n = jn * np_
    tm = _tile(m, 1024, 128)
    res_bytes = 0 if res is None else res.dtype.itemsize
    tn = _fit_tile(np_, MAX_TILE_COLS, lambda t: 2 * (tm * k * a.dtype.itemsize + k * t * b.dtype.itemsize
                                                       + tm * t * (jnp.dtype(out_dtype).itemsize + res_bytes)))
    nb = np_ // tn

    def body(*refs):
        a_ref, b_ref = refs[0], refs[1]
        o_ref = refs[-1]
        acc = jnp.dot(a_ref[...].astype(BF16), b_ref[...].astype(BF16), preferred_element_type=F32)
        if res is not None:
            acc = acc + refs[2][...].astype(F32)
        o_ref[...] = acc.astype(o_ref.dtype)

    if b.ndim == 3:
        b_spec = pl.BlockSpec((None, k, tn), lambda i, j: (j // nb, 0, j % nb))
    else:
        b_spec = pl.BlockSpec((k, tn), lambda i, j: (0, j))
    in_specs = [pl.BlockSpec((tm, k), lambda i, j: (i, 0)), b_spec]
    args = [a, b]
    if res is not None:
        in_specs.append(pl.BlockSpec((tm, tn), lambda i, j: (i, j)))
        args.append(res)
    return pl.pallas_call(
        body, name=name, grid=(m // tm, n // tn), in_specs=in_specs,
        out_specs=pl.BlockSpec((tm, tn), lambda i, j: (i, j)),
        out_shape=jax.ShapeDtypeStruct((m, n), out_dtype),
        compiler_params=_params(("parallel", "arbitrary")))(*args)


def _mm_nt(a, b, out_dtype, name):
    m, n = a.shape
    jn, np_ = _col_blocks(b)
    k = b.shape[-2]
    tm = _tile(m, 1024, 128)
    tk = _fit_tile(k, 1024, lambda t: 2 * (tm * n * a.dtype.itemsize + t * n * b.dtype.itemsize
                                            + tm * t * jnp.dtype(out_dtype).itemsize))

    def body(a_ref, b_ref, o_ref):
        acc = None
        for j in range(jn):
            bj = b_ref[j] if b.ndim == 3 else b_ref[...]
            part = lax.dot_general(a_ref[:, j * np_:(j + 1) * np_].astype(BF16), bj.astype(BF16),
                                   (((1,), (1,)), ((), ())), preferred_element_type=F32)
            acc = part if acc is None else acc + part
        o_ref[...] = acc.astype(o_ref.dtype)

    if b.ndim == 3:
        b_spec = pl.BlockSpec((jn, tk, np_), lambda i, j: (0, j, 0))
    else:
        b_spec = pl.BlockSpec((tk, n), lambda i, j: (j, 0))
    return pl.pallas_call(
        body, name=name, grid=(m // tm, k // tk),
        in_specs=[pl.BlockSpec((tm, n), lambda i, j: (i, 0)), b_spec],
        out_specs=pl.BlockSpec((tm, tk), lambda i, j: (i, j)),
        out_shape=jax.ShapeDtypeStruct((m, k), out_dtype),
        compiler_params=_params(("parallel", "arbitrary")))(a, b)


def _mm_tn(a, b, like, name):
    m, k = a.shape
    _, n = b.shape
    jn, np_ = _col_blocks(like)
    tk = _tile(k, 512, LANES)
    tn = _fit_tile(np_, MAX_TILE_COLS, lambda t: 2 * (m * tk * a.dtype.itemsize + m * t * b.dtype.itemsize
                                                       + tk * t * like.dtype.itemsize))
    nb = np_ // tn

    def body(a_ref, b_ref, o_ref):
        acc = lax.dot_general(a_ref[...].astype(BF16), b_ref[...].astype(BF16),
                              (((0,), (0,)), ((), ())), preferred_element_type=F32)
        o_ref[...] = acc.astype(o_ref.dtype)

    if like.ndim == 3:
        out_spec = pl.BlockSpec((None, tk, tn), lambda i, j: (j // nb, i, j % nb))
    else:
        out_spec = pl.BlockSpec((tk, tn), lambda i, j: (i, j))
    return pl.pallas_call(
        body, name=name, grid=(k // tk, n // tn),
        in_specs=[pl.BlockSpec((m, tk), lambda i, j: (0, i)), pl.BlockSpec((m, tn), lambda i, j: (0, j))],
        out_specs=out_spec, out_shape=jax.ShapeDtypeStruct(like.shape, like.dtype),
        compiler_params=_params(("parallel", "arbitrary")))(a, b)


def _matmul_op(out_dtype, with_res, tag):
    def fwd_only(a, w, res):
        return _mm_nn(a, w, res if with_res else None, out_dtype, f"mm_{tag}")

    @jax.custom_vjp
    def op(a, w, res):
        return fwd_only(a, w, res)

    def fwd(a, w, res):
        return fwd_only(a, w, res), (a, w, res)

    def bwd(saved, dc):
        a, w, res = saved
        da = _mm_nt(dc, w, a.dtype, f"mm_{tag}_da")
        dw = _mm_tn(a, dc, w, f"mm_{tag}_dw")
        dres = dc.astype(res.dtype) if with_res else jnp.zeros_like(res)
        return da, dw, dres

    op.defvjp(fwd, bwd)
    return op


def matmul(a, w, out_dtype, tag):
    return _matmul_op(out_dtype, False, tag)(a, w, jnp.zeros((), F32))


def matmul_res(a, w, res, tag):
    return _matmul_op(res.dtype, True, tag)(a, w, res)


def _rms_fwd_call(x, gain, groups, out_dtype, name):
    s, w = x.shape
    gw = w // groups
    ts = _tile(s, 512, 128)

    def body(x_ref, g_ref, o_ref):
        for g in range(groups):
            cols = slice(g * gw, (g + 1) * gw)
            xv = x_ref[:, cols].astype(F32)
            r = lax.rsqrt(jnp.mean(xv * xv, axis=1, keepdims=True) + EPS)
            o_ref[:, cols] = (xv * r * g_ref[:, cols]).astype(o_ref.dtype)

    return pl.pallas_call(
        body, name=name, grid=(s // ts,),
        in_specs=[pl.BlockSpec((ts, w), lambda i: (i, 0)), pl.BlockSpec((1, w), lambda i: (0, 0))],
        out_specs=pl.BlockSpec((ts, w), lambda i: (i, 0)),
        out_shape=jax.ShapeDtypeStruct((s, w), out_dtype),
        compiler_params=_params(("parallel",)))(x, gain.reshape(1, w))


def _rms_bwd_call(x, gain, dy, groups, name):
    s, w = x.shape
    gw = w // groups
    ts = _tile(s, 512, 128)

    def body(x_ref, g_ref, dy_ref, dx_ref, dg_ref):
        @pl.when(pl.program_id(0) == 0)
        def _():
            dg_ref[...] = jnp.zeros_like(dg_ref)

        for g in range(groups):
            cols = slice(g * gw, (g + 1) * gw)
            xv = x_ref[:, cols].astype(F32)
            dyv = dy_ref[:, cols].astype(F32)
            r = lax.rsqrt(jnp.mean(xv * xv, axis=1, keepdims=True) + EPS)
            xhat = xv * r
            gdy = dyv * g_ref[:, cols]
            dx = r * (gdy - xhat * jnp.mean(gdy * xhat, axis=1, keepdims=True))
            dx_ref[:, cols] = dx.astype(dx_ref.dtype)
            dg_ref[:, cols] += jnp.sum(dyv * xhat, axis=0, keepdims=True)

    return pl.pallas_call(
        body, name=name, grid=(s // ts,),
        in_specs=[pl.BlockSpec((ts, w), lambda i: (i, 0)), pl.BlockSpec((1, w), lambda i: (0, 0)),
                  pl.BlockSpec((ts, w), lambda i: (i, 0))],
        out_specs=[pl.BlockSpec((ts, w), lambda i: (i, 0)), pl.BlockSpec((1, w), lambda i: (0, 0))],
        out_shape=[jax.ShapeDtypeStruct((s, w), x.dtype), jax.ShapeDtypeStruct((1, w), F32)],
        compiler_params=_params(("arbitrary",)))(x, gain.reshape(1, w), dy)


def rms_norm(x, gain, groups, out_dtype, tag):
    @jax.custom_vjp
    def op(x, gain):
        return _rms_fwd_call(x, gain, groups, out_dtype, f"rms_{tag}")

    def fwd(x, gain):
        return _rms_fwd_call(x, gain, groups, out_dtype, f"rms_{tag}"), (x, gain)

    def bwd(saved, dy):
        x, gain = saved
        dx, dg = _rms_bwd_call(x, gain, dy, groups, f"rms_{tag}_bwd")
        return dx, dg.reshape(gain.shape)

    op.defvjp(fwd, bwd)
    return op(x, gain)


def _rope_call(x, tables, odd_chunks_only, transpose, out_dtype, name):
    s, w = x.shape
    ts = _tile(s, 512, 128)
    cos_t, sin_lo, sin_hi, half = tables
    one_roll = 2 * half == LANES

    def body(x_ref, c_ref, lo_ref, hi_ref, o_ref):
        c, s_lo, s_hi = c_ref[...], lo_ref[...], hi_ref[...]
        for j in range(w // LANES):
            cols = slice(j * LANES, (j + 1) * LANES)
            if odd_chunks_only and j % 2 == 0:
                o_ref[:, cols] = x_ref[:, cols].astype(o_ref.dtype)
                continue
            xv = x_ref[:, cols].astype(F32)
            if one_roll:
                y = xv * c + (pltpu.roll(xv * (s_lo + s_hi), half, 1) if transpose
                              else pltpu.roll(xv, half, 1) * (s_lo + s_hi))
            elif transpose:
                y = xv * c + pltpu.roll(xv * s_lo, LANES - half, 1) + pltpu.roll(xv * s_hi, half, 1)
            else:
                y = xv * c + pltpu.roll(xv, half, 1) * s_lo + pltpu.roll(xv, LANES - half, 1) * s_hi
            o_ref[:, cols] = y.astype(o_ref.dtype)

    row = pl.BlockSpec((ts, LANES), lambda i: (i, 0))
    return pl.pallas_call(
        body, name=name, grid=(s // ts,),
        in_specs=[pl.BlockSpec((ts, w), lambda i: (i, 0)), row, row, row],
        out_specs=pl.BlockSpec((ts, w), lambda i: (i, 0)),
        out_shape=jax.ShapeDtypeStruct((s, w), out_dtype),
        compiler_params=_params(("parallel",)))(x, cos_t, sin_lo, sin_hi)


def rope(x, tables, odd_chunks_only, out_dtype, tag):
    @jax.custom_vjp
    def op(x):
        return _rope_call(x, tables, odd_chunks_only, False, out_dtype, f"rope_{tag}")

    def fwd(x):
        return _rope_call(x, tables, odd_chunks_only, False, out_dtype, f"rope_{tag}"), None

    def bwd(_, dy):
        return (_rope_call(dy, tables, odd_chunks_only, True, x.dtype, f"rope_{tag}_bwd"),)

    op.defvjp(fwd, bwd)
    return op(x)


def rope_tables(seq, dim):
    half = dim // 2
    pos = jnp.arange(seq, dtype=F32)
    inv_freq = ROPE_THETA ** (-jnp.arange(0, dim, 2, dtype=F32) / dim)
    ang = pos[:, None] * inv_freq[None, :]
    cos, sin = jnp.cos(ang), jnp.sin(ang)
    zero = jnp.zeros_like(sin)
    rest = jnp.zeros((seq, LANES - dim), F32)
    return (jnp.concatenate([cos, cos, rest], axis=1), jnp.concatenate([zero, sin, rest], axis=1),
            jnp.concatenate([-sin, zero, rest], axis=1), half)


def _attn_blocks(s):
    return min(256, s), min(512, s)


def _sb_blocks(s):
    b = min(256, s)
    return b, b


def _score_mask(delta, mode):
    if mode != "dilated":
        return None, delta >= 0
    mult = jnp.zeros(delta.shape, F32)
    for window, dilation in DILATED_PAIRS:
        hit = delta <= window
        if dilation > 1:
            hit = hit & ((delta & (dilation - 1)) == 0)
        mult = mult + jnp.where(hit, 1.0, 0.0)
    bias = jnp.where(mult > 2.5, math.log(3.0), jnp.where(mult > 1.5, math.log(2.0), 0.0))
    return bias, (delta >= 0) & (mult > 0.5)


def _kv_range(qi, bq, bk, mode):
    hi = (qi * bq + bq - 1) // bk
    if mode == "dilated":
        lo = jnp.maximum(qi * bq - MAX_WINDOW, 0) // bk
        return lo, lo, hi
    return 0, (qi * bq + 1) // bk, hi


def _attn_fwd_call(q, k, v, cq, ck, heads, scale, mode, name):
    s = q.shape[0]
    dk, dv = q.shape[1] // heads, v.shape[1] // heads
    bq, bk = _attn_blocks(s)
    nq, nk = s // bq, s // bk
    fox = mode == "fox"

    def body(*refs):
        q_ref, k_ref, v_ref = refs[:3]
        o_ref, o32_ref, lse_ref = refs[-3:]
        qi = pl.program_id(1)
        qv = q_ref[...].astype(BF16)
        row_col = (lax.broadcasted_iota(jnp.int32, (bq, bk), 0) - lax.broadcasted_iota(jnp.int32, (bq, bk), 1))
        lo, full, hi = _kv_range(qi, bq, bk, mode)

        def step(masked, kj, carry):
            m, l, acc = carry
            rows = pl.ds(pl.multiple_of(kj * bk, bk), bk)
            kb = k_ref[rows, :].astype(BF16)
            vb = v_ref[rows, :].astype(BF16)
            sc = lax.dot_general(qv, kb, (((1,), (1,)), ((), ())), preferred_element_type=F32) * scale
            if fox:
                sc = sc + refs[3][...] - refs[4][kj]
            if masked:
                bias, mask = _score_mask(row_col + (qi * bq - kj * bk), mode)
                if bias is not None:
                    sc = sc + bias
                sc = jnp.where(mask, sc, NEG_INF)
            m_new = jnp.maximum(m, jnp.max(sc, axis=1, keepdims=True))
            alpha = jnp.exp(m - m_new)
            p = jnp.exp(sc - m_new)
            l = l * alpha + jnp.sum(p, axis=1, keepdims=True)
            acc = acc * alpha + jnp.dot(p.astype(BF16), vb, preferred_element_type=F32)
            return m_new, l, acc

        carry = (jnp.full((bq, 1), NEG_INF, F32), jnp.zeros((bq, 1), F32), jnp.zeros((bq, dv), F32))
        carry = lax.fori_loop(lo, full, functools.partial(step, False), carry)
        m, l, acc = lax.fori_loop(full, hi + 1, functools.partial(step, True), carry)
        out = acc / l
        o_ref[...] = out.astype(o_ref.dtype)
        o32_ref[...] = out
        lse_ref[...] = m + jnp.log(l)

    in_specs = [pl.BlockSpec((bq, dk), lambda h, i: (i, h)), pl.BlockSpec((s, dk), lambda h, i: (0, h)),
                pl.BlockSpec((s, dv), lambda h, i: (0, h))]
    args = [q, k, v]
    if fox:
        in_specs += [pl.BlockSpec((None, bq, 1), lambda h, i: (h, i, 0)),
                     pl.BlockSpec((None, nk, 1, bk), lambda h, i: (h, 0, 0, 0))]
        args += [cq, ck]
    return pl.pallas_call(
        body, name=name, grid=(heads, nq), in_specs=in_specs,
        out_specs=[pl.BlockSpec((bq, dv), lambda h, i: (i, h)), pl.BlockSpec((bq, dv), lambda h, i: (i, h)),
                   pl.BlockSpec((None, bq, 1), lambda h, i: (h, i, 0))],
        out_shape=[jax.ShapeDtypeStruct((s, heads * dv), BF16), jax.ShapeDtypeStruct((s, heads * dv), F32),
                   jax.ShapeDtypeStruct((heads, s, 1), F32)],
        compiler_params=_params(("parallel", "arbitrary")))(*args)


def _attn_bwd_call(q, k, v, cq, ck, o, lse, do, heads, scale, mode, name):
    s = q.shape[0]
    dk, dv = q.shape[1] // heads, v.shape[1] // heads
    bq, bk = _attn_blocks(s)
    nq, nk = s // bq, s // bk
    fox = mode == "fox"
    n_in = 8 if fox else 6

    def body(*refs):
        q_ref, k_ref, v_ref = refs[:3]
        o_ref, lse_ref, do_ref = refs[n_in - 3:n_in]
        outs = refs[n_in:]
        dq_ref, dk_ref, dv_ref = outs[:3]
        scratch = outs[5:] if fox else outs[3:]
        dk_acc, dv_acc = scratch[:2]
        qi = pl.program_id(1)

        @pl.when(qi == 0)
        def _():
            dk_acc[...] = jnp.zeros_like(dk_acc)
            dv_acc[...] = jnp.zeros_like(dv_acc)
            if fox:
                scratch[2][...] = jnp.zeros_like(scratch[2])

        qv = q_ref[...].astype(BF16)
        dov = do_ref[...].astype(BF16)
        delta_o = jnp.sum(dov.astype(F32) * o_ref[...], axis=1, keepdims=True)
        lse = lse_ref[...]
        row_col = (lax.broadcasted_iota(jnp.int32, (bq, bk), 0) - lax.broadcasted_iota(jnp.int32, (bq, bk), 1))
        lo, full, hi = _kv_range(qi, bq, bk, mode)

        def step(masked, kj, carry):
            dq, dcq = carry
            rows = pl.ds(pl.multiple_of(kj * bk, bk), bk)
            kb = k_ref[rows, :].astype(BF16)
            vb = v_ref[rows, :].astype(BF16)
            sc = lax.dot_general(qv, kb, (((1,), (1,)), ((), ())), preferred_element_type=F32) * scale
            if fox:
                sc = sc + refs[3][...] - refs[4][kj]
            if masked:
                bias, mask = _score_mask(row_col + (qi * bq - kj * bk), mode)
                if bias is not None:
                    sc = sc + bias
                sc = jnp.where(mask, sc, NEG_INF)
            p = jnp.exp(sc - lse)
            dv_acc[rows, :] += lax.dot_general(p.astype(BF16), dov, (((0,), (0,)), ((), ())),
                                               preferred_element_type=F32)
            dp = lax.dot_general(dov, vb, (((1,), (1,)), ((), ())), preferred_element_type=F32)
            ds = p * (dp - delta_o)
            dsb = ds.astype(BF16)
            dk_acc[rows, :] += lax.dot_general(dsb, qv, (((0,), (0,)), ((), ())), preferred_element_type=F32)
            if fox:
                scratch[2][kj] -= jnp.sum(ds, axis=0, keepdims=True)
                dcq = dcq + jnp.sum(ds, axis=1, keepdims=True)
            return dq + jnp.dot(dsb, kb, preferred_element_type=F32), dcq

        carry = (jnp.zeros((bq, dk), F32), jnp.zeros((bq, 1), F32))
        carry = lax.fori_loop(lo, full, functools.partial(step, False), carry)
        dq, dcq = lax.fori_loop(full, hi + 1, functools.partial(step, True), carry)
        dq_ref[...] = (dq * scale).astype(dq_ref.dtype)
        if fox:
            outs[4][...] = dcq

        @pl.when(qi == nq - 1)
        def _():
            dk_ref[...] = (dk_acc[...] * scale).astype(dk_ref.dtype)
            dv_ref[...] = dv_acc[...].astype(dv_ref.dtype)
            if fox:
                outs[3][...] = scratch[2][...]

    in_specs = [pl.BlockSpec((bq, dk), lambda h, i: (i, h)), pl.BlockSpec((s, dk), lambda h, i: (0, h)),
                pl.BlockSpec((s, dv), lambda h, i: (0, h))]
    args = [q, k, v]
    if fox:
        in_specs += [pl.BlockSpec((None, bq, 1), lambda h, i: (h, i, 0)),
                     pl.BlockSpec((None, nk, 1, bk), lambda h, i: (h, 0, 0, 0))]
        args += [cq, ck]
    in_specs += [pl.BlockSpec((bq, dv), lambda h, i: (i, h)), pl.BlockSpec((None, bq, 1), lambda h, i: (h, i, 0)),
                 pl.BlockSpec((bq, dv), lambda h, i: (i, h))]
    args += [o, lse, do]
    out_specs = [pl.BlockSpec((bq, dk), lambda h, i: (i, h)), pl.BlockSpec((s, dk), lambda h, i: (0, h)),
                 pl.BlockSpec((s, dv), lambda h, i: (0, h))]
    out_shape = [jax.ShapeDtypeStruct(q.shape, q.dtype), jax.ShapeDtypeStruct(k.shape, k.dtype),
                 jax.ShapeDtypeStruct(v.shape, v.dtype)]
    scratch_shapes = [pltpu.VMEM((s, dk), F32), pltpu.VMEM((s, dv), F32)]
    if fox:
        out_specs.append(pl.BlockSpec((None, nk, 1, bk), lambda h, i: (h, 0, 0, 0)))
        out_shape.append(jax.ShapeDtypeStruct((heads, nk, 1, bk), F32))
        out_specs.append(pl.BlockSpec((None, bq, 1), lambda h, i: (h, i, 0)))
        out_shape.append(jax.ShapeDtypeStruct((heads, s, 1), F32))
        scratch_shapes.append(pltpu.VMEM((nk, 1, bk), F32))
    return pl.pallas_call(
        body, name=name, grid=(heads, nq), in_specs=in_specs, out_specs=out_specs, out_shape=out_shape,
        scratch_shapes=scratch_shapes, compiler_params=_params(("arbitrary", "arbitrary")))(*args)


def attention(q, k, v, cum, heads, scale, mode, tag):
    s = q.shape[0]
    _, bk = _attn_blocks(s)

    def layouts(cum):
        return cum.reshape(heads, s, 1), cum.reshape(heads, s // bk, 1, bk)

    @jax.custom_vjp
    def op(q, k, v, cum):
        cq, ck = layouts(cum)
        return _attn_fwd_call(q, k, v, cq, ck, heads, scale, mode, f"attn_{tag}")[0]

    def fwd(q, k, v, cum):
        cq, ck = layouts(cum)
        o, o32, lse = _attn_fwd_call(q, k, v, cq, ck, heads, scale, mode, f"attn_{tag}")
        return o, (q, k, v, cum, o32, lse)

    def bwd(saved, do):
        q, k, v, cum, o, lse = saved
        cq, ck = layouts(cum)
        res = _attn_bwd_call(q, k, v, cq, ck, o, lse, do, heads, scale, mode, f"attn_{tag}_bwd")
        dcum = res[3].reshape(heads, s) + res[4].reshape(heads, s) if mode == "fox" else jnp.zeros_like(cum)
        return res[0], res[1], res[2], dcum

    op.defvjp(fwd, bwd)
    return op(q, k, v, cum)


def _log_sigmoid(z):
    return jnp.minimum(z, 0.0) - jnp.log(1.0 + jnp.exp(-jnp.abs(z)))


def _tri_dot(x, tri):
    hi = x.astype(BF16)
    lo = (x - hi.astype(F32)).astype(BF16)
    return jnp.dot(hi, tri, preferred_element_type=F32) + jnp.dot(lo, tri, preferred_element_type=F32)


def _sb_call(q, k, v, keep_all, do, heads, scale, backward, name):
    s = q.shape[0]
    d = q.shape[1] // heads
    bq, bk = _sb_blocks(s)
    nq = s // bq

    def body(*refs):
        q_ref, k_ref, v_ref = refs[:3]
        qi = pl.program_id(1)
        qv = q_ref[...].astype(BF16)
        q_pos = qi * bq + lax.broadcasted_iota(jnp.int32, (bq, bk), 0)
        k_col = lax.broadcasted_iota(jnp.int32, (bq, bk), 1)
        r_i = lax.broadcasted_iota(jnp.int32, (bk, bk), 0)
        c_i = lax.broadcasted_iota(jnp.int32, (bk, bk), 1)
        hi = (qi * bq + bq - 1) // bk

        def logits(kj):
            rows = pl.ds(pl.multiple_of(kj * bk, bk), bk)
            kb = k_ref[rows, :].astype(BF16)
            z = lax.dot_general(qv, kb, (((1,), (1,)), ((), ())), preferred_element_type=F32) * scale
            past = (kj * bk + k_col) < q_pos
            ls = _log_sigmoid(z)
            lk = jnp.where(past, ls - z, 0.0)
            return rows, kb, z, past, ls, lk

        if not backward:
            o_ref, keep_ref = refs[3:5]
            after = jnp.where(r_i > c_i, 1.0, 0.0).astype(BF16)

            def step(it, carry):
                keep_right, acc = carry
                rows, _, _, past, ls, lk = logits(hi - it)
                a = jnp.where(past, jnp.exp(ls + _tri_dot(lk, after) + keep_right), 0.0)
                acc = acc + jnp.dot(a.astype(BF16), v_ref[rows, :].astype(BF16), preferred_element_type=F32)
                return keep_right + jnp.sum(lk, axis=1, keepdims=True), acc

            keep, acc = lax.fori_loop(0, hi + 1, step, (jnp.zeros((bq, 1), F32), jnp.zeros((bq, d), F32)))
            o_ref[...] = acc.astype(o_ref.dtype)
            keep_ref[...] = keep
            return

        keep_ref, do_ref, dq_ref, dk_ref, dv_ref, dk_acc, dv_acc = refs[3:]

        @pl.when(qi == 0)
        def _():
            dk_acc[...] = jnp.zeros_like(dk_acc)
            dv_acc[...] = jnp.zeros_like(dv_acc)

        dov = do_ref[...].astype(BF16)
        keep_total = keep_ref[...]
        upto = jnp.where(r_i <= c_i, 1.0, 0.0).astype(BF16)
        before = jnp.where(r_i < c_i, 1.0, 0.0).astype(BF16)

        def step(kj, carry):
            keep_left, g_left, dq = carry
            rows, kb, z, past, ls, lk = logits(kj)
            between = keep_total - (keep_left + _tri_dot(lk, upto))
            a = jnp.where(past, jnp.exp(ls + between), 0.0)
            vb = v_ref[rows, :].astype(BF16)
            da = lax.dot_general(dov, vb, (((1,), (1,)), ((), ())), preferred_element_type=F32)
            g = a * da
            g_before = g_left + _tri_dot(g, before)
            dz = jnp.where(past, g * jnp.exp(ls - z) - jnp.exp(ls) * g_before, 0.0)
            dzb = dz.astype(BF16)
            dk_acc[rows, :] += lax.dot_general(dzb, qv, (((0,), (0,)), ((), ())), preferred_element_type=F32)
            dv_acc[rows, :] += lax.dot_general(a.astype(BF16), dov, (((0,), (0,)), ((), ())),
                                               preferred_element_type=F32)
            dq = dq + jnp.dot(dzb, kb, preferred_element_type=F32)
            return (keep_left + jnp.sum(lk, axis=1, keepdims=True),
                    g_left + jnp.sum(g, axis=1, keepdims=True), dq)

        init = (jnp.zeros((bq, 1), F32), jnp.zeros((bq, 1), F32), jnp.zeros((bq, d), F32))
        _, _, dq = lax.fori_loop(0, hi + 1, step, init)
        dq_ref[...] = (dq * scale).astype(dq_ref.dtype)

        @pl.when(qi == nq - 1)
        def _():
            dk_ref[...] = (dk_acc[...] * scale).astype(dk_ref.dtype)
            dv_ref[...] = dv_acc[...].astype(dv_ref.dtype)

    blk_q = pl.BlockSpec((bq, d), lambda h, i: (i, h))
    blk_kv = pl.BlockSpec((s, d), lambda h, i: (0, h))
    blk_row = pl.BlockSpec((None, bq, 1), lambda h, i: (h, i, 0))
    if not backward:
        return pl.pallas_call(
            body, name=name, grid=(heads, nq), in_specs=[blk_q, blk_kv, blk_kv], out_specs=[blk_q, blk_row],
            out_shape=[jax.ShapeDtypeStruct(q.shape, BF16), jax.ShapeDtypeStruct((heads, s, 1), F32)],
            compiler_params=_params(("parallel", "arbitrary")))(q, k, v)
    return pl.pallas_call(
        body, name=name, grid=(heads, nq), in_specs=[blk_q, blk_kv, blk_kv, blk_row, blk_q],
        out_specs=[blk_q, blk_kv, blk_kv],
        out_shape=[jax.ShapeDtypeStruct(q.shape, q.dtype), jax.ShapeDtypeStruct(k.shape, k.dtype),
                   jax.ShapeDtypeStruct(v.shape, v.dtype)],
        scratch_shapes=[pltpu.VMEM((s, d), F32), pltpu.VMEM((s, d), F32)],
        compiler_params=_params(("arbitrary", "arbitrary")))(q, k, v, keep_all, do)


def stick_breaking(q, k, v, heads, scale, tag):
    @jax.custom_vjp
    def op(q, k, v):
        return _sb_call(q, k, v, None, None, heads, scale, False, f"sb_{tag}")[0]

    def fwd(q, k, v):
        o, keep_all = _sb_call(q, k, v, None, None, heads, scale, False, f"sb_{tag}")
        return o, (q, k, v, keep_all)

    def bwd(saved, do):
        q, k, v, keep_all = saved
        return tuple(_sb_call(q, k, v, keep_all, do, heads, scale, True, f"sb_{tag}_bwd"))

    op.defvjp(fwd, bwd)
    return op(q, k, v)


def _tri_dot3(x, tri):
    hi = x.astype(BF16)
    r1 = x - hi.astype(F32)
    mid = r1.astype(BF16)
    lo = (r1 - mid.astype(F32)).astype(BF16)
    return (jnp.dot(hi, tri, preferred_element_type=F32) + jnp.dot(mid, tri, preferred_element_type=F32)
            + jnp.dot(lo, tri, preferred_element_type=F32))


def _gate_call(f, bias_b, dcum, name):
    heads, r, _ = f.shape
    backward = dcum is not None

    def body(*refs):
        f_ref, b_ref = refs[:2]
        r_i = lax.broadcasted_iota(jnp.int32, (LANES, LANES), 0)
        c_i = lax.broadcasted_iota(jnp.int32, (LANES, LANES), 1)
        x = f_ref[...] + b_ref[...]
        if not backward:
            o_ref, ls_ref = refs[2:]
            ls_ref[...] = _log_sigmoid(x)
            o_ref[...] = _tri_dot3(ls_ref[...], jnp.where(r_i <= c_i, 1.0, 0.0).astype(BF16))

            def row(i, carry):
                o_ref[pl.ds(i, 1), :] = o_ref[pl.ds(i, 1), :] + carry
                return carry + jnp.sum(ls_ref[pl.ds(i, 1), :], axis=1, keepdims=True)

            lax.fori_loop(0, r, row, jnp.zeros((1, 1), F32))
            return

        dc_ref, df_ref, db_ref, acc_ref = refs[2:]
        acc_ref[...] = _tri_dot3(dc_ref[...], jnp.where(r_i >= c_i, 1.0, 0.0).astype(BF16))

        def row(it, carry):
            i = r - 1 - it
            acc_ref[pl.ds(i, 1), :] = acc_ref[pl.ds(i, 1), :] + carry
            return carry + jnp.sum(dc_ref[pl.ds(i, 1), :], axis=1, keepdims=True)

        lax.fori_loop(0, r, row, jnp.zeros((1, 1), F32))
        df = acc_ref[...] * jnp.exp(_log_sigmoid(-x))
        df_ref[...] = df
        lane = lax.broadcasted_iota(jnp.int32, (1, LANES), 1)
        db_ref[...] = jnp.where(lane == 0, jnp.sum(df), 0.0)

    blk = pl.BlockSpec((None, r, LANES), lambda h: (h, 0, 0))
    blk_b = pl.BlockSpec((None, 1, LANES), lambda h: (h, 0, 0))
    if not backward:
        return pl.pallas_call(
            body, name=name, grid=(heads,), in_specs=[blk, blk_b], out_specs=blk,
            out_shape=jax.ShapeDtypeStruct(f.shape, F32), scratch_shapes=[pltpu.VMEM((r, LANES), F32)],
            compiler_params=_params(("parallel",)))(f, bias_b)
    return pl.pallas_call(
        body, name=name, grid=(heads,), in_specs=[blk, blk_b, blk], out_specs=[blk, blk_b],
        out_shape=[jax.ShapeDtypeStruct(f.shape, F32), jax.ShapeDtypeStruct(bias_b.shape, F32)],
        scratch_shapes=[pltpu.VMEM((r, LANES), F32)],
        compiler_params=_params(("parallel",)))(f, bias_b, dcum)


def forget_gate_cumsum(f, bias_b, tag):
    @jax.custom_vjp
    def op(f, bias_b):
        return _gate_call(f, bias_b, None, f"gate_{tag}")

    def fwd(f, bias_b):
        return _gate_call(f, bias_b, None, f"gate_{tag}"), (f, bias_b)

    def bwd(saved, dcum):
        return tuple(_gate_call(saved[0], saved[1], dcum, f"gate_{tag}_bwd"))

    op.defvjp(fwd, bwd)
    return op(f, bias_b)


def _swiglu_call(g, u, dact, name):
    s, n = g.shape
    ts = _tile(s, 256, 128)
    backward = dact is not None

    def body(*refs):
        gv = refs[0][...].astype(F32)
        uv = refs[1][...].astype(F32)
        sig = 1.0 / (1.0 + jnp.exp(-gv))
        if not backward:
            refs[2][...] = (gv * sig * uv).astype(refs[2].dtype)
            return
        dv = refs[2][...].astype(F32)
        refs[3][...] = (dv * uv * sig * (1.0 + gv * (1.0 - sig))).astype(refs[3].dtype)
        refs[4][...] = (dv * gv * sig).astype(refs[4].dtype)

    blk = pl.BlockSpec((ts, n), lambda i: (i, 0))
    shape = jax.ShapeDtypeStruct((s, n), g.dtype)
    if not backward:
        return pl.pallas_call(body, name=name, grid=(s // ts,), in_specs=[blk, blk], out_specs=blk,
                              out_shape=shape, compiler_params=_params(("parallel",)))(g, u)
    return pl.pallas_call(body, name=name, grid=(s // ts,), in_specs=[blk, blk, blk], out_specs=[blk, blk],
                          out_shape=[shape, shape], compiler_params=_params(("parallel",)))(g, u, dact)


def swiglu(g, u, tag):
    @jax.custom_vjp
    def op(g, u):
        return _swiglu_call(g, u, None, f"swiglu_{tag}")

    def fwd(g, u):
        return _swiglu_call(g, u, None, f"swiglu_{tag}"), (g, u)

    def bwd(saved, dact):
        return tuple(_swiglu_call(saved[0], saved[1], dact, f"swiglu_{tag}_bwd"))

    op.defvjp(fwd, bwd)
    return op(g, u)


def final_norm_loss(x, gain, target):
    s, d = x.shape
    ts = _tile(s, 256, 128)

    def body(x_ref, g_ref, t_ref, sq_ref, dx_ref, dg_ref):
        @pl.when(pl.program_id(0) == 0)
        def _():
            sq_ref[...] = jnp.zeros_like(sq_ref)
            dg_ref[...] = jnp.zeros_like(dg_ref)

        xv = x_ref[...]
        r = lax.rsqrt(jnp.mean(xv * xv, axis=1, keepdims=True) + EPS)
        xhat = xv * r
        err = xhat * g_ref[...] - t_ref[...]
        sq_ref[...] += jnp.sum(err * err)
        dy = err * (1.0 / d)
        gdy = dy * g_ref[...]
        dx_ref[...] = r * (gdy - xhat * jnp.mean(gdy * xhat, axis=1, keepdims=True))
        dg_ref[...] += jnp.sum(dy * xhat, axis=0, keepdims=True)

    blk = pl.BlockSpec((ts, d), lambda i: (i, 0))
    row = pl.BlockSpec((1, d), lambda i: (0, 0))
    return pl.pallas_call(
        body, name="final_norm_loss", grid=(s // ts,), in_specs=[blk, row, blk],
        out_specs=[pl.BlockSpec((1, LANES), lambda i: (0, 0)), blk, row],
        out_shape=[jax.ShapeDtypeStruct((1, LANES), F32), jax.ShapeDtypeStruct((s, d), F32),
                   jax.ShapeDtypeStruct((1, d), F32)],
        compiler_params=_params(("arbitrary",)))(x, gain.reshape(1, d), target)


def adamw(w, g, m, v, name):
    rows, cols = w.shape
    tr = _tile(rows, 256, 8)
    c1 = 1.0 - ADAM_B1 ** ADAM_STEP
    c2 = 1.0 - ADAM_B2 ** ADAM_STEP

    def body(w_ref, g_ref, m_ref, v_ref, d_ref, nm_ref, nv_ref):
        gv = g_ref[...]
        m_new = ADAM_B1 * m_ref[...] + (1.0 - ADAM_B1) * gv
        v_new = ADAM_B2 * v_ref[...] + (1.0 - ADAM_B2) * (gv * gv)
        d_ref[...] = -ADAM_LR * ((m_new / c1) / (jnp.sqrt(v_new / c2) + ADAM_EPS) + ADAM_WD * w_ref[...])
        nm_ref[...] = m_new
        nv_ref[...] = v_new

    blk = pl.BlockSpec((tr, cols), lambda i: (i, 0))
    shape = jax.ShapeDtypeStruct((rows, cols), F32)
    return pl.pallas_call(body, name=name, grid=(rows // tr,), in_specs=[blk] * 4, out_specs=[blk] * 3,
                          out_shape=[shape] * 3, compiler_params=_params(("parallel",)))(w, g, m, v)


ANY = pl.BlockSpec(memory_space=pl.ANY)


def _position():
    return lax.axis_index("x"), lax.axis_index("y"), lax.axis_index("c")


HBM = pl.BlockSpec(memory_space=pltpu.HBM)
SEM = pl.BlockSpec(memory_space=pltpu.SEMAPHORE)
DATAFLOW = pltpu.SideEffectType.DATAFLOW_SIDE_EFFECTING


def _gather_copies(srcs, lands, send_sems, recv_sems):
    x, y, c = _position()
    chips = [(1 - x, y), (x, 1 - y), (1 - x, 1 - y)]
    return [pltpu.make_async_remote_copy(
        src_ref=srcs[a], dst_ref=lands[a].at[2 * x + y], send_sem=send_sems.at[3 * a + k], recv_sem=recv_sems.at[3 * a + k],
        device_id=(chips[k][0], chips[k][1], c), device_id_type=MESH) for a in range(len(srcs)) for k in range(3)]


def _scatter_copies(srcs, lands, send_sems, recv_sems):
    x, y, c = _position()
    out = []
    for a in range(len(srcs)):
        for k in range(1, 8):
            px, py, pc = (1 - x if k & 4 else x, 1 - y if k & 2 else y, 1 - c if k & 1 else c)
            out.append(pltpu.make_async_remote_copy(
                src_ref=srcs[a].at[2 * px + py, pc], dst_ref=lands[a].at[4 * x + 2 * y + c],
                send_sem=send_sems.at[7 * a + k - 1], recv_sem=recv_sems.at[7 * a + k - 1],
                device_id=(px, py, pc), device_id_type=MESH))
    return out


def _swap_copies(srcs, lands, send_sems, recv_sems):
    x, y, c = _position()
    return [pltpu.make_async_remote_copy(
        src_ref=srcs[a], dst_ref=lands[a].at[c], send_sem=send_sems.at[a], recv_sem=recv_sems.at[a],
        device_id=(x, y, 1 - c), device_id_type=MESH) for a in range(len(srcs))]


EXCHANGES = {"gather": (_gather_copies, 3), "scatter": (_scatter_copies, 7), "swap": (_swap_copies, 1)}


def start_exchange(kind, srcs, lands, after, name):
    copies, per_array = EXCHANGES[kind]
    n = len(srcs)

    def body(*refs):
        for cp in copies(refs[:n], refs[n:2 * n], refs[2 * n + 1], refs[2 * n + 2]):
            cp.start()
        refs[-1][...] = jnp.zeros_like(refs[-1])

    sems = pltpu.SemaphoreType.DMA((n * per_array,))
    hbm = [pltpu.HBM(a.shape, a.dtype) for a in list(srcs) + list(lands)]
    outs = pl.pallas_call(
        body, name=name, in_specs=[HBM] * (2 * n) + [ANY],
        out_shape=(sems, sems, *hbm, jax.ShapeDtypeStruct((8, LANES), F32)),
        out_specs=(SEM, SEM, *[HBM] * (2 * n), pl.BlockSpec(memory_space=pltpu.VMEM)),
        input_output_aliases={i: 2 + i for i in range(2 * n)},
        compiler_params=pltpu.CompilerParams(has_side_effects=DATAFLOW),
    )(*[pltpu.with_memory_space_constraint(a, pltpu.HBM) for a in list(srcs) + list(lands)], after)
    return (kind, n, outs[:-1]), outs[-1]


def finish_exchange(state, after, name):
    kind, n, (send_sems, recv_sems, *buffers) = state
    copies, _ = EXCHANGES[kind]

    def body(*refs):
        for cp in copies(refs[:n], refs[n:2 * n], refs[2 * n], refs[2 * n + 1]):
            cp.wait_send()
            cp.wait_recv()

    outs = pl.pallas_call(
        body, name=name, in_specs=[HBM] * (2 * n) + [SEM, SEM, ANY],
        out_shape=[pltpu.HBM(a.shape, a.dtype) for a in buffers], out_specs=[HBM] * (2 * n),
        input_output_aliases={i: i for i in range(2 * n)},
        compiler_params=pltpu.CompilerParams(has_side_effects=DATAFLOW),
    )(*buffers, send_sems, recv_sems, after)
    return outs[n:]


def _own_slot(land_shape, dtype, block, slot):
    start = (slot,) + (0,) * block.ndim
    return lax.dynamic_update_slice(lax.empty(land_shape, dtype), block[None], start)


def broadcast_all(block, name):
    r, lanes = block.shape

    def body(b_ref, out_ref, send_sems, recv_sems, local_sem):
        x, y, c = _position()
        me = 4 * x + 2 * y + c
        mine = pltpu.make_async_copy(b_ref, out_ref.at[me], local_sem)
        mine.start()

        def copy(k, to_me):
            px, py, pc = (1 - x if k & 4 else x, 1 - y if k & 2 else y, 1 - c if k & 1 else c)
            slot = 4 * px + 2 * py + pc if to_me else me
            return pltpu.make_async_remote_copy(
                src_ref=b_ref, dst_ref=out_ref.at[slot], send_sem=send_sems.at[k - 1],
                recv_sem=recv_sems.at[k - 1], device_id=(px, py, pc), device_id_type=MESH)

        sends = [copy(k, False) for k in range(1, 8)]
        for cp in sends:
            cp.start()
        for k in range(1, 8):
            copy(k, True).wait_recv()
        for cp in sends:
            cp.wait_send()
        mine.wait()

    return pl.pallas_call(
        body, name=name, in_specs=[ANY], out_specs=ANY,
        out_shape=jax.ShapeDtypeStruct((8, r, lanes), block.dtype),
        scratch_shapes=[pltpu.SemaphoreType.DMA((7,)), pltpu.SemaphoreType.DMA((7,)), pltpu.SemaphoreType.DMA])(block)


def sum_slots(stack, name):
    n, r, lanes = stack.shape
    tr = _tile(r, max(16, (1 << 22) // (n * lanes * stack.dtype.itemsize) // 16 * 16), 16)

    def body(s_ref, o_ref):
        acc = s_ref[0].astype(F32)
        for i in range(1, n):
            acc = acc + s_ref[i].astype(F32)
        o_ref[...] = acc

    return pl.pallas_call(
        body, name=name, grid=(r // tr,), in_specs=[pl.BlockSpec((n, tr, lanes), lambda i: (0, i, 0))],
        out_specs=pl.BlockSpec((tr, lanes), lambda i: (i, 0)), out_shape=jax.ShapeDtypeStruct((r, lanes), F32),
        compiler_params=_params(("parallel",)))(stack)


def _pack(arrays, dtype, row_align):
    flat = jnp.concatenate([a.reshape(-1).astype(dtype) for a in arrays])
    rows = -(-flat.shape[0] // LANES)
    rows = -(-rows // row_align) * row_align
    return jnp.pad(flat, (0, rows * LANES - flat.shape[0])).reshape(rows, LANES)


def _unpack(flat, shapes):
    flat = flat.reshape(flat.shape[:-2] + (-1,))
    out, at = [], 0
    for shp in shapes:
        n = math.prod(shp)
        out.append(flat[..., at:at + n].reshape(flat.shape[:-1] + tuple(shp)))
        at += n
    return out


def _pad_cols(w):
    return jnp.pad(w, ((0, 0), (0, -w.shape[1] % LANES)))


def split_columns(raw, width, widths, dtypes):
    wp = raw.shape[1] * width // sum(widths)
    jn = raw.shape[1] // wp

    def cut(raw):
        whole = jnp.concatenate([raw[:, j * wp:j * wp + width] for j in range(jn)], axis=1)
        out, at = [], 0
        for n, dt in zip(widths, dtypes):
            out.append(whole[:, at:at + n].astype(dt))
            at += n
        return tuple(out)

    @jax.custom_vjp
    def op(raw):
        return cut(raw)

    def fwd(raw):
        return cut(raw), None

    def bwd(_, cts):
        whole = jnp.concatenate([ct.astype(BF16) for ct in cts], axis=1)
        gap = jnp.zeros((whole.shape[0], wp - width), BF16)
        blocks = []
        for j in range(jn):
            blocks += [whole[:, j * width:(j + 1) * width], gap]
        return (jnp.concatenate(blocks, axis=1).astype(raw.dtype),)

    op.defvjp(fwd, bwd)
    return op(raw)


def _mixer_half(x, mats, smalls, tables, widths, heads, tag):
    s, d_model = x.shape
    gw = heads * HEAD_DIM
    w_in, w_uq, w_ukv, w_out = mats
    attn_norm, q_norm, kv_norm, f_bias, group_norm = smalls
    rope_full, rope_mla = tables
    no_cum = jnp.zeros((heads, s), F32)

    h = rms_norm(x, attn_norm, 1, BF16, f"attn_{tag}")
    raw = matmul(h, w_in, F32, f"in_{tag}")
    sizes = [Q_LORA, KV_LORA, QK_ROPE, 2 * gw, gw, gw, gw, gw, heads, gw, gw, gw]
    kinds = [F32, F32, F32, F32, BF16, BF16, BF16, BF16, F32, BF16, BF16, BF16]
    q_lat, kv_lat, k_rope, qk_b, v_b, q_c, k_c, v_c, f_logit, q_d, k_d, v_d = split_columns(
        raw, widths[0], sizes, kinds)

    q_all = matmul(rms_norm(q_lat, q_norm, 1, BF16, f"qlat_{tag}"), w_uq, BF16, f"uq_{tag}")
    kv_all = matmul(rms_norm(kv_lat, kv_norm, 1, BF16, f"kvlat_{tag}"), w_ukv, BF16, f"ukv_{tag}")
    q_a = rope(q_all, rope_mla, True, BF16, f"qpe_{tag}")
    k_pe = rope(_pad_cols(k_rope), rope_mla, False, BF16, f"kpe_{tag}")
    kv4 = kv_all.reshape(s, heads, 2, LANES)
    k_a = jnp.stack([kv4[:, :, 0], jnp.broadcast_to(k_pe[:, None, :], (s, heads, LANES))], axis=2).reshape(s, 2 * gw)
    out_a = attention(q_a, k_a, kv4[:, :, 1].reshape(s, gw), no_cum, heads, (QK_NOPE + QK_ROPE) ** -0.5,
                      "causal", f"a_{tag}")

    qk_b = rope(qk_b, rope_full, False, BF16, f"qkb_{tag}")
    out_b = attention(qk_b[:, :gw], qk_b[:, gw:], v_b, no_cum, heads, HEAD_DIM ** -0.5, "dilated", f"b_{tag}")

    f_rows = f_logit.T.reshape(heads, s // LANES, LANES)
    bias_b = jnp.broadcast_to(f_bias[:, None, None], (heads, 1, LANES))
    cum = forget_gate_cumsum(f_rows, bias_b, tag).reshape(heads, s)
    out_c = attention(q_c, k_c, v_c, cum, heads, HEAD_DIM ** -0.5, "fox", f"c_{tag}")

    out_d = stick_breaking(q_d, k_d, v_d, heads, HEAD_DIM ** -0.5, f"d_{tag}")

    groups = jnp.concatenate([out_a, out_b, out_c, out_d], axis=1)
    return matmul_res(rms_norm(groups, group_norm, 4, BF16, f"group_{tag}"), w_out.reshape(-1, d_model), x, f"out_{tag}")


def _ffn_half(x, mats, smalls, tag):
    d_model = x.shape[1]
    w_gate, w_up, w_down = mats
    h2 = rms_norm(x, smalls[0], 1, BF16, f"ffn_{tag}")
    act = swiglu(matmul(h2, w_gate, BF16, f"gate_{tag}"), matmul(h2, w_up, BF16, f"up_{tag}"), tag)
    return matmul_res(act, w_down.reshape(-1, d_model), x, f"down_{tag}")


def kernel(x, attn_norm, w_in, mla_q_norm, w_uq, mla_kv_norm, w_ukv, fox_forget_bias, group_norm, w_out, ffn_norm, w_gate, w_up, w_down, final_norm, loss_target, m_attn_norm, m_w_in, m_mla_q_norm, m_w_uq, m_mla_kv_norm, m_w_ukv, m_fox_forget_bias, m_group_norm, m_w_out, m_ffn_norm, m_w_gate, m_w_up, m_w_down, m_final_norm, v_attn_norm, v_w_in, v_mla_q_norm, v_w_uq, v_mla_kv_norm, v_w_ukv, v_fox_forget_bias, v_group_norm, v_w_out, v_ffn_norm, v_w_gate, v_w_up, v_w_down, v_final_norm):
    depth = w_in.shape[0]
    _, s, d_model = x.shape
    heads = d_model // 4 // HEAD_DIM
    big = [w_in, w_uq, w_ukv, w_out, w_gate, w_up, w_down]
    big_m = [m_w_in, m_w_uq, m_w_ukv, m_w_out, m_w_gate, m_w_up, m_w_down]
    big_v = [v_w_in, v_w_uq, v_w_ukv, v_w_out, v_w_gate, v_w_up, v_w_down]
    small = [attn_norm, mla_q_norm, mla_kv_norm, fox_forget_bias, group_norm, ffn_norm]
    small_m = [m_attn_norm, m_mla_q_norm, m_mla_kv_norm, m_fox_forget_bias, m_group_norm, m_ffn_norm, m_final_norm]
    small_v = [v_attn_norm, v_mla_q_norm, v_mla_kv_norm, v_fox_forget_bias, v_group_norm, v_ffn_norm, v_final_norm]
    widths = [w.shape[2] for w in big]
    tables = (rope_tables(s, HEAD_DIM), rope_tables(s, QK_ROPE))

    px, py, pc = _position()

    n_units = 2 * depth
    mixer_w, mixer_s = 4, 5

    def unit_arrays(u, arrays, split):
        l, ffn = divmod(u, 2)
        return [a[l] for a in (arrays[split:] if ffn else arrays[:split])]

    def unit_fn(u):
        tag = str(u // 2)
        if u % 2:
            return lambda g, sm, xx: _ffn_half(xx, g, sm, tag)
        return lambda g, sm, xx: _mixer_half(xx, g, sm, tables, widths, heads, tag)

    def start_gather(u, after):
        shards = [_pad_cols(w).astype(BF16) for w in unit_arrays(u, big, mixer_w)]
        lands = [_own_slot((4,) + sh.shape, BF16, sh, 2 * px + py) for sh in shards]
        return start_exchange("gather", shards, lands, after, f"gather_start_{u}")

    state, token = start_gather(0, x)
    gathered = finish_exchange(state, token, "gather_wait_0")
    x_l, vjps = x[0], []
    for u in range(n_units):
        small_u = unit_arrays(u, small, mixer_s)
        if u + 1 < n_units:
            state, token = start_gather(u + 1, gathered[0])
            small_u[0] = small_u[0] + token[0, 0]
        x_l, vjp_u = jax.vjp(unit_fn(u), gathered, small_u, x_l)
        vjps.append(vjp_u)
        if u + 1 < n_units:
            gathered = finish_exchange(state, x_l, f"gather_wait_{u + 1}")

    sq, dx, d_final = final_norm_loss(x_l, final_norm, loss_target[0])
    loss = 0.5 / d_model * lax.psum(sq[0, 0], ("x", "y", "c"))

    d_small_units = [None] * n_units
    grad_units = [None] * n_units
    scatters, swaps = {}, {}

    def finish_scatter(u, after):
        recv = finish_exchange(scatters.pop(u), after, f"scatter_wait_{u}")
        halves = [sum_slots(r, f"sum_{u}_{i}") for i, r in enumerate(recv)]
        lands = [_own_slot((2,) + h.shape, F32, h, pc) for h in halves]
        swaps[u], token = start_exchange("swap", halves, lands, after, f"swap_start_{u}")
        return token

    def finish_swap(u, after):
        full = finish_exchange(swaps.pop(u), after, f"swap_wait_{u}")
        unit_widths = widths[mixer_w:] if u % 2 else widths[:mixer_w]
        grad_units[u] = [f.reshape(-1, f.shape[2])[:, :wd] for f, wd in zip(full, unit_widths)]

    for u in reversed(range(n_units)):
        d_g, d_small_units[u], dx = vjps[u](dx)
        parts = [g.reshape(4, 2, g.shape[1] // 2, g.shape[2]) for g in d_g]
        lands = [_own_slot((8,) + p.shape[2:], BF16, p[2 * px + py, pc], 4 * px + 2 * py + pc) for p in parts]
        scatters[u], token = start_exchange("scatter", parts, lands, dx, f"scatter_start_{u}")
        tokens = token[0, 0]
        if u + 2 < n_units:
            finish_swap(u + 2, dx)
        if u + 1 < n_units:
            tokens = tokens + finish_scatter(u + 1, dx)[0, 0]
        dx = dx + tokens
    finish_swap(1, dx)
    token = finish_scatter(0, dx)
    finish_swap(0, token)
    big_g = [jnp.stack([(grad_units[2 * l] + grad_units[2 * l + 1])[i] for l in range(depth)])
             for i in range(len(big))]
    d_small = [jnp.stack([(list(d_small_units[2 * l]) + list(d_small_units[2 * l + 1]))[i] for l in range(depth)])
               for i in range(len(small))]

    small_all = small + [final_norm]
    packed = _pack(d_small + [d_final], F32, 8)
    small_sum = sum_slots(broadcast_all(packed, "small_gather"), "small_sum")
    small_g = _unpack(small_sum, [p.shape for p in small_all])

    def update(w, g, m, v, name):
        shp = w.shape
        two_d = (-1, shp[-1])
        return [o.reshape(shp) for o in adamw(w.reshape(two_d), g.reshape(two_d), m.reshape(two_d),
                                              v.reshape(two_d), name)]

    big_u = [update(w, g, m, v, f"adamw_big{i}") for i, (w, g, m, v) in enumerate(zip(big, big_g, big_m, big_v))]
    small_u = adamw(_pack(small_all, F32, 8), small_sum, _pack(small_m, F32, 8), _pack(small_v, F32, 8), "adamw_small")
    small_u = [_unpack(u, [p.shape for p in small_all]) for u in small_u]

    def ordered(bigs, smalls):
        a_n, q_n, kv_n, f_b, g_n, f_n, fin = smalls
        wi, uq, ukv, wo, wg, wu, wd = bigs
        return [a_n, wi, q_n, uq, kv_n, ukv, f_b, g_n, wo, f_n, wg, wu, wd, fin]

    grads = ordered(big_g, small_g)
    deltas = ordered([u[0] for u in big_u], small_u[0])
    new_m = ordered([u[1] for u in big_u], small_u[1])
    new_v = ordered([u[2] for u in big_u], small_u[2])
    return (loss, dx[None], *grads, *deltas, *new_m, *new_v)
```

```python
import functools
import math

import jax
import jax.numpy as jnp
from jax import lax
from jax.experimental import pallas as pl
from jax.experimental.pallas import tpu as pltpu

F32 = jnp.float32
BF16 = jnp.bfloat16
MESH = pl.DeviceIdType.MESH

HEAD_DIM = 128
Q_LORA = 512
KV_LORA = 512
QK_NOPE = 128
QK_ROPE = 64
DILATED_PAIRS = ((128, 1), (512, 4), (2048, 16))
MAX_WINDOW = max(w for w, _ in DILATED_PAIRS)
ROPE_THETA = 10000.0
EPS = 1e-6
NEG_INF = -1e30
ADAM_LR = 0.001
ADAM_B1 = 0.9
ADAM_B2 = 0.999
ADAM_EPS = 1e-08
ADAM_WD = 0.01
ADAM_STEP = 10

LANES = 128
VMEM_LIMIT_BYTES = 56 * 1024 * 1024


def _params(semantics=None):
    return pltpu.CompilerParams(dimension_semantics=semantics, vmem_limit_bytes=VMEM_LIMIT_BYTES)


MATMUL_VMEM_BUDGET = 40 * 1024 * 1024
MAX_TILE_COLS = 1536


def _fit_tile(n, cap, estimate):
    t = _tile(n, cap, LANES)
    while estimate(t) > MATMUL_VMEM_BUDGET and t > LANES:
        t = _tile(n, t - LANES, LANES)
    return t


def _tile(n, cap, align):
    if n <= cap:
        return n
    t = cap - cap % align
    while t >= align:
        if n % t == 0:
            return t
        t -= align
    raise ValueError(f"no tile for {n} under {cap}")


def _col_blocks(b):
    return (b.shape[0], b.shape[2]) if b.ndim == 3 else (1, b.shape[1])


def _mm_nn(a, b, res, out_dtype, name):
    m, k = a.shape
    jn, np_ = _col_blocks(b)
    n = jn * np_
    tm = _tile(m, 1024, 128)
    res_bytes = 0 if res is None else res.dtype.itemsize
    tn = _fit_tile(np_, MAX_TILE_COLS, lambda t: 2 * (tm * k * a.dtype.itemsize + k * t * b.dtype.itemsize
                                                       + tm * t * (jnp.dtype(out_dtype).itemsize + res_bytes)))
    nb = np_ // tn

    def body(*refs):
        a_ref, b_ref = refs[0], refs[1]
        o_ref = refs[-1]
        acc = jnp.dot(a_ref[...].astype(BF16), b_ref[...].astype(BF16), preferred_element_type=F32)
        if res is not None:
            acc = acc + refs[2][...].astype(F32)
        o_ref[...] = acc.astype(o_ref.dtype)

    if b.ndim == 3:
        b_spec = pl.BlockSpec((None, k, tn), lambda i, j: (j // nb, 0, j % nb))
    else:
        b_spec = pl.BlockSpec((k, tn), lambda i, j: (0, j))
    in_specs = [pl.BlockSpec((tm, k), lambda i, j: (i, 0)), b_spec]
    args = [a, b]
    if res is not None:
        in_specs.append(pl.BlockSpec((tm, tn), lambda i, j: (i, j)))
        args.append(res)
    return pl.pallas_call(
        body, name=name, grid=(m // tm, n // tn), in_specs=in_specs,
        out_specs=pl.BlockSpec((tm, tn), lambda i, j: (i, j)),
        out_shape=jax.ShapeDtypeStruct((m, n), out_dtype),
        compiler_params=_params(("parallel", "arbitrary")))(*args)


def _mm_nt(a, b, out_dtype, name):
    m, n = a.shape
    jn, np_ = _col_blocks(b)
    k = b.shape[-2]
    tm = _tile(m, 1024, 128)
    tk = _fit_tile(k, 1024, lambda t: 2 * (tm * n * a.dtype.itemsize + t * n * b.dtype.itemsize
                                            + tm * t * jnp.dtype(out_dtype).itemsize))

    def body(a_ref, b_ref, o_ref):
        acc = None
        for j in range(jn):
            bj = b_ref[j] if b.ndim == 3 else b_ref[...]
            part = lax.dot_general(a_ref[:, j * np_:(j + 1) * np_].astype(BF16), bj.astype(BF16),
                                   (((1,), (1,)), ((), ())), preferred_element_type=F32)
            acc = part if acc is None else acc + part
        o_ref[...] = acc.astype(o_ref.dtype)

    if b.ndim == 3:
        b_spec = pl.BlockSpec((jn, tk, np_), lambda i, j: (0, j, 0))
    else:
        b_spec = pl.BlockSpec((tk, n), lambda i, j: (j, 0))
    return pl.pallas_call(
        body, name=name, grid=(m // tm, k // tk),
        in_specs=[pl.BlockSpec((tm, n), lambda i, j: (i, 0)), b_spec],
        out_specs=pl.BlockSpec((tm, tk), lambda i, j: (i, j)),
        out_shape=jax.ShapeDtypeStruct((m, k), out_dtype),
        compiler_params=_params(("parallel", "arbitrary")))(a, b)


def _mm_tn(a, b, like, name):
    m, k = a.shape
    _, n = b.shape
    jn, np_ = _col_blocks(like)
    tk = _tile(k, 512, LANES)
    tn = _fit_tile(np_, MAX_TILE_COLS, lambda t: 2 * (m * tk * a.dtype.itemsize + m * t * b.dtype.itemsize
                                                       + tk * t * like.dtype.itemsize))
    nb = np_ // tn

    def body(a_ref, b_ref, o_ref):
        acc = lax.dot_general(a_ref[...].astype(BF16), b_ref[...].astype(BF16),
                              (((0,), (0,)), ((), ())), preferred_element_type=F32)
        o_ref[...] = acc.astype(o_ref.dtype)

    if like.ndim == 3:
        out_spec = pl.BlockSpec((None, tk, tn), lambda i, j: (j // nb, i, j % nb))
    else:
        out_spec = pl.BlockSpec((tk, tn), lambda i, j: (i, j))
    return pl.pallas_call(
        body, name=name, grid=(k // tk, n // tn),
        in_specs=[pl.BlockSpec((m, tk), lambda i, j: (0, i)), pl.BlockSpec((m, tn), lambda i, j: (0, j))],
        out_specs=out_spec, out_shape=jax.ShapeDtypeStruct(like.shape, like.dtype),
        compiler_params=_params(("parallel", "arbitrary")))(a, b)


def _matmul_op(out_dtype, with_res, tag):
    def fwd_only(a, w, res):
        return _mm_nn(a, w, res if with_res else None, out_dtype, f"mm_{tag}")

    @jax.custom_vjp
    def op(a, w, res):
        return fwd_only(a, w, res)

    def fwd(a, w, res):
        return fwd_only(a, w, res), (a, w, res)

    def bwd(saved, dc):
        a, w, res = saved
        if not with_res and dc.dtype != BF16:
            dc = dc.astype(BF16)
        da = _mm_nt(dc, w, a.dtype, f"mm_{tag}_da")
        dw = _mm_tn(a, dc, w, f"mm_{tag}_dw")
        dres = dc.astype(res.dtype) if with_res else jnp.zeros_like(res)
        return da, dw, dres

    op.defvjp(fwd, bwd)
    return op


def matmul(a, w, out_dtype, tag):
    return _matmul_op(out_dtype, False, tag)(a, w, jnp.zeros((), F32))


def matmul_res(a, w, res, tag):
    return _matmul_op(res.dtype, True, tag)(a, w, res)


def _rms_fwd_call(x, gain, groups, out_dtype, name):
    s, w = x.shape
    gw = w // groups
    ts = _tile(s, 512, 128)

    def body(x_ref, g_ref, o_ref):
        for g in range(groups):
            cols = slice(g * gw, (g + 1) * gw)
            xv = x_ref[:, cols].astype(F32)
            r = lax.rsqrt(jnp.mean(xv * xv, axis=1, keepdims=True) + EPS)
            o_ref[:, cols] = (xv * r * g_ref[:, cols]).astype(o_ref.dtype)

    return pl.pallas_call(
        body, name=name, grid=(s // ts,),
        in_specs=[pl.BlockSpec((ts, w), lambda i: (i, 0)), pl.BlockSpec((1, w), lambda i: (0, 0))],
        out_specs=pl.BlockSpec((ts, w), lambda i: (i, 0)),
        out_shape=jax.ShapeDtypeStruct((s, w), out_dtype),
        compiler_params=_params(("parallel",)))(x, gain.reshape(1, w))


def _rms_bwd_call(x, gain, dy, groups, name):
    s, w = x.shape
    gw = w // groups
    ts = _tile(s, 512, 128)

    def body(x_ref, g_ref, dy_ref, dx_ref, dg_ref):
        @pl.when(pl.program_id(0) == 0)
        def _():
            dg_ref[...] = jnp.zeros_like(dg_ref)

        for g in range(groups):
            cols = slice(g * gw, (g + 1) * gw)
            xv = x_ref[:, cols].astype(F32)
            dyv = dy_ref[:, cols].astype(F32)
            r = lax.rsqrt(jnp.mean(xv * xv, axis=1, keepdims=True) + EPS)
            xhat = xv * r
            gdy = dyv * g_ref[:, cols]
            dx = r * (gdy - xhat * jnp.mean(gdy * xhat, axis=1, keepdims=True))
            dx_ref[:, cols] = dx.astype(dx_ref.dtype)
            dg_ref[:, cols] += jnp.sum(dyv * xhat, axis=0, keepdims=True)

    return pl.pallas_call(
        body, name=name, grid=(s // ts,),
        in_specs=[pl.BlockSpec((ts, w), lambda i: (i, 0)), pl.BlockSpec((1, w), lambda i: (0, 0)),
                  pl.BlockSpec((ts, w), lambda i: (i, 0))],
        out_specs=[pl.BlockSpec((ts, w), lambda i: (i, 0)), pl.BlockSpec((1, w), lambda i: (0, 0))],
        out_shape=[jax.ShapeDtypeStruct((s, w), x.dtype), jax.ShapeDtypeStruct((1, w), F32)],
        compiler_params=_params(("arbitrary",)))(x, gain.reshape(1, w), dy)


def rms_norm(x, gain, groups, out_dtype, tag):
    @jax.custom_vjp
    def op(x, gain):
        return _rms_fwd_call(x, gain, groups, out_dtype, f"rms_{tag}")

    def fwd(x, gain):
        return _rms_fwd_call(x, gain, groups, out_dtype, f"rms_{tag}"), (x, gain)

    def bwd(saved, dy):
        x, gain = saved
        dx, dg = _rms_bwd_call(x, gain, dy, groups, f"rms_{tag}_bwd")
        return dx, dg.reshape(gain.shape)

    op.defvjp(fwd, bwd)
    return op(x, gain)


def _rope_call(x, tables, odd_chunks_only, transpose, out_dtype, name):
    s, w = x.shape
    ts = _tile(s, 512, 128)
    cos_t, sin_lo, sin_hi, half = tables
    one_roll = 2 * half == LANES

    def body(x_ref, c_ref, lo_ref, hi_ref, o_ref):
        c, s_lo, s_hi = c_ref[...], lo_ref[...], hi_ref[...]
        for j in range(w // LANES):
            cols = slice(j * LANES, (j + 1) * LANES)
            if odd_chunks_only and j % 2 == 0:
                o_ref[:, cols] = x_ref[:, cols].astype(o_ref.dtype)
                continue
            xv = x_ref[:, cols].astype(F32)
            if one_roll:
                y = xv * c + (pltpu.roll(xv * (s_lo + s_hi), half, 1) if transpose
                              else pltpu.roll(xv, half, 1) * (s_lo + s_hi))
            elif transpose:
                y = xv * c + pltpu.roll(xv * s_lo, LANES - half, 1) + pltpu.roll(xv * s_hi, half, 1)
            else:
                y = xv * c + pltpu.roll(xv, half, 1) * s_lo + pltpu.roll(xv, LANES - half, 1) * s_hi
            o_ref[:, cols] = y.astype(o_ref.dtype)

    row = pl.BlockSpec((ts, LANES), lambda i: (i, 0))
    return pl.pallas_call(
        body, name=name, grid=(s // ts,),
        in_specs=[pl.BlockSpec((ts, w), lambda i: (i, 0)), row, row, row],
        out_specs=pl.BlockSpec((ts, w), lambda i: (i, 0)),
        out_shape=jax.ShapeDtypeStruct((s, w), out_dtype),
        compiler_params=_params(("parallel",)))(x, cos_t, sin_lo, sin_hi)


def rope(x, tables, odd_chunks_only, out_dtype, tag):
    @jax.custom_vjp
    def op(x):
        return _rope_call(x, tables, odd_chunks_only, False, out_dtype, f"rope_{tag}")

    def fwd(x):
        return _rope_call(x, tables, odd_chunks_only, False, out_dtype, f"rope_{tag}"), None

    def bwd(_, dy):
        return (_rope_call(dy, tables, odd_chunks_only, True, x.dtype, f"rope_{tag}_bwd"),)

    op.defvjp(fwd, bwd)
    return op(x)


def rope_tables(seq, dim):
    half = dim // 2
    pos = jnp.arange(seq, dtype=F32)
    inv_freq = ROPE_THETA ** (-jnp.arange(0, dim, 2, dtype=F32) / dim)
    ang = pos[:, None] * inv_freq[None, :]
    cos, sin = jnp.cos(ang), jnp.sin(ang)
    zero = jnp.zeros_like(sin)
    rest = jnp.zeros((seq, LANES - dim), F32)
    return (jnp.concatenate([cos, cos, rest], axis=1), jnp.concatenate([zero, sin, rest], axis=1),
            jnp.concatenate([-sin, zero, rest], axis=1), half)


def _attn_blocks(s):
    return min(256, s), min(512, s)


def _sb_blocks(s):
    b = min(256, s)
    return b, b


def _score_mask(delta, mode):
    if mode != "dilated":
        return None, delta >= 0
    mult = jnp.zeros(delta.shape, F32)
    for window, dilation in DILATED_PAIRS:
        hit = delta <= window
        if dilation > 1:
            hit = hit & ((delta & (dilation - 1)) == 0)
        mult = mult + jnp.where(hit, 1.0, 0.0)
    bias = jnp.where(mult > 2.5, math.log(3.0), jnp.where(mult > 1.5, math.log(2.0), 0.0))
    return bias, (delta >= 0) & (mult > 0.5)


def _kv_range(qi, bq, bk, mode):
    hi = (qi * bq + bq - 1) // bk
    if mode == "dilated":
        lo = jnp.maximum(qi * bq - MAX_WINDOW, 0) // bk
        return lo, lo, hi
    return 0, (qi * bq + 1) // bk, hi


def _attn_fwd_call(q, k, v, cq, ck, heads, scale, mode, name):
    s = q.shape[0]
    dk, dv = q.shape[1] // heads, v.shape[1] // heads
    bq, bk = _attn_blocks(s)
    nq, nk = s // bq, s // bk
    fox = mode == "fox"

    def body(*refs):
        q_ref, k_ref, v_ref = refs[:3]
        o_ref, o32_ref, lse_ref = refs[-3:]
        qi = pl.program_id(1)
        qv = q_ref[...].astype(BF16)
        row_col = (lax.broadcasted_iota(jnp.int32, (bq, bk), 0) - lax.broadcasted_iota(jnp.int32, (bq, bk), 1))
        lo, full, hi = _kv_range(qi, bq, bk, mode)

        def step(masked, kj, carry):
            m, l, acc = carry
            rows = pl.ds(pl.multiple_of(kj * bk, bk), bk)
            kb = k_ref[rows, :].astype(BF16)
            vb = v_ref[rows, :].astype(BF16)
            sc = lax.dot_general(qv, kb, (((1,), (1,)), ((), ())), preferred_element_type=F32) * scale
            if fox:
                sc = sc + refs[3][...] - refs[4][kj]
            if masked:
                bias, mask = _score_mask(row_col + (qi * bq - kj * bk), mode)
                if bias is not None:
                    sc = sc + bias
                sc = jnp.where(mask, sc, NEG_INF)
            m_new = jnp.maximum(m, jnp.max(sc, axis=1, keepdims=True))
            alpha = jnp.exp(m - m_new)
            p = jnp.exp(sc - m_new)
            l = l * alpha + jnp.sum(p, axis=1, keepdims=True)
            acc = acc * alpha + jnp.dot(p.astype(BF16), vb, preferred_element_type=F32)
            return m_new, l, acc

        carry = (jnp.full((bq, 1), NEG_INF, F32), jnp.zeros((bq, 1), F32), jnp.zeros((bq, dv), F32))
        carry = lax.fori_loop(lo, full, functools.partial(step, False), carry)
        m, l, acc = lax.fori_loop(full, hi + 1, functools.partial(step, True), carry)
        out = acc / l
        o_ref[...] = out.astype(o_ref.dtype)
        o32_ref[...] = out
        lse_ref[...] = m + jnp.log(l)

    in_specs = [pl.BlockSpec((bq, dk), lambda h, i: (i, h)), pl.BlockSpec((s, dk), lambda h, i: (0, h)),
                pl.BlockSpec((s, dv), lambda h, i: (0, h))]
    args = [q, k, v]
    if fox:
        in_specs += [pl.BlockSpec((None, bq, 1), lambda h, i: (h, i, 0)),
                     pl.BlockSpec((None, nk, 1, bk), lambda h, i: (h, 0, 0, 0))]
        args += [cq, ck]
    return pl.pallas_call(
        body, name=name, grid=(heads, nq), in_specs=in_specs,
        out_specs=[pl.BlockSpec((bq, dv), lambda h, i: (i, h)), pl.BlockSpec((bq, dv), lambda h, i: (i, h)),
                   pl.BlockSpec((None, bq, 1), lambda h, i: (h, i, 0))],
        out_shape=[jax.ShapeDtypeStruct((s, heads * dv), BF16), jax.ShapeDtypeStruct((s, heads * dv), F32),
                   jax.ShapeDtypeStruct((heads, s, 1), F32)],
        compiler_params=_params(("parallel", "arbitrary")))(*args)


def _attn_bwd_call(q, k, v, cq, ck, o, lse, do, heads, scale, mode, name):
    s = q.shape[0]
    dk, dv = q.shape[1] // heads, v.shape[1] // heads
    bq, bk = _attn_blocks(s)
    nq, nk = s // bq, s // bk
    fox = mode == "fox"
    n_in = 8 if fox else 6

    def body(*refs):
        q_ref, k_ref, v_ref = refs[:3]
        o_ref, lse_ref, do_ref = refs[n_in - 3:n_in]
        outs = refs[n_in:]
        dq_ref, dk_ref, dv_ref = outs[:3]
        scratch = outs[5:] if fox else outs[3:]
        dk_acc, dv_acc = scratch[:2]
        qi = pl.program_id(1)

        @pl.when(qi == 0)
        def _():
            dk_acc[...] = jnp.zeros_like(dk_acc)
            dv_acc[...] = jnp.zeros_like(dv_acc)
            if fox:
                scratch[2][...] = jnp.zeros_like(scratch[2])

        qv = q_ref[...].astype(BF16)
        dov = do_ref[...].astype(BF16)
        delta_o = jnp.sum(dov.astype(F32) * o_ref[...], axis=1, keepdims=True)
        lse = lse_ref[...]
        row_col = (lax.broadcasted_iota(jnp.int32, (bq, bk), 0) - lax.broadcasted_iota(jnp.int32, (bq, bk), 1))
        lo, full, hi = _kv_range(qi, bq, bk, mode)

        def step(masked, kj, carry):
            dq, dcq = carry
            rows = pl.ds(pl.multiple_of(kj * bk, bk), bk)
            kb = k_ref[rows, :].astype(BF16)
            vb = v_ref[rows, :].astype(BF16)
            sc = lax.dot_general(qv, kb, (((1,), (1,)), ((), ())), preferred_element_type=F32) * scale
            if fox:
                sc = sc + refs[3][...] - refs[4][kj]
            if masked:
                bias, mask = _score_mask(row_col + (qi * bq - kj * bk), mode)
                if bias is not None:
                    sc = sc + bias
                sc = jnp.where(mask, sc, NEG_INF)
            p = jnp.exp(sc - lse)
            dv_acc[rows, :] += lax.dot_general(p.astype(BF16), dov, (((0,), (0,)), ((), ())),
                                               preferred_element_type=F32)
            dp = lax.dot_general(dov, vb, (((1,), (1,)), ((), ())), preferred_element_type=F32)
            ds = p * (dp - delta_o)
            dsb = ds.astype(BF16)
            dk_acc[rows, :] += lax.dot_general(dsb, qv, (((0,), (0,)), ((), ())), preferred_element_type=F32)
            if fox:
                scratch[2][kj] -= jnp.sum(ds, axis=0, keepdims=True)
                dcq = dcq + jnp.sum(ds, axis=1, keepdims=True)
            return dq + jnp.dot(dsb, kb, preferred_element_type=F32), dcq

        carry = (jnp.zeros((bq, dk), F32), jnp.zeros((bq, 1), F32))
        carry = lax.fori_loop(lo, full, functools.partial(step, False), carry)
        dq, dcq = lax.fori_loop(full, hi + 1, functools.partial(step, True), carry)
        dq_ref[...] = (dq * scale).astype(dq_ref.dtype)
        if fox:
            outs[4][...] = dcq

        @pl.when(qi == nq - 1)
        def _():
            dk_ref[...] = (dk_acc[...] * scale).astype(dk_ref.dtype)
            dv_ref[...] = dv_acc[...].astype(dv_ref.dtype)
            if fox:
                outs[3][...] = scratch[2][...]

    in_specs = [pl.BlockSpec((bq, dk), lambda h, i: (i, h)), pl.BlockSpec((s, dk), lambda h, i: (0, h)),
                pl.BlockSpec((s, dv), lambda h, i: (0, h))]
    args = [q, k, v]
    if fox:
        in_specs += [pl.BlockSpec((None, bq, 1), lambda h, i: (h, i, 0)),
                     pl.BlockSpec((None, nk, 1, bk), lambda h, i: (h, 0, 0, 0))]
        args += [cq, ck]
    in_specs += [pl.BlockSpec((bq, dv), lambda h, i: (i, h)), pl.BlockSpec((None, bq, 1), lambda h, i: (h, i, 0)),
                 pl.BlockSpec((bq, dv), lambda h, i: (i, h))]
    args += [o, lse, do]
    out_specs = [pl.BlockSpec((bq, dk), lambda h, i: (i, h)), pl.BlockSpec((s, dk), lambda h, i: (0, h)),
                 pl.BlockSpec((s, dv), lambda h, i: (0, h))]
    out_shape = [jax.ShapeDtypeStruct(q.shape, q.dtype), jax.ShapeDtypeStruct(k.shape, k.dtype),
                 jax.ShapeDtypeStruct(v.shape, v.dtype)]
    scratch_shapes = [pltpu.VMEM((s, dk), F32), pltpu.VMEM((s, dv), F32)]
    if fox:
        out_specs.append(pl.BlockSpec((None, nk, 1, bk), lambda h, i: (h, 0, 0, 0)))
        out_shape.append(jax.ShapeDtypeStruct((heads, nk, 1, bk), F32))
        out_specs.append(pl.BlockSpec((None, bq, 1), lambda h, i: (h, i, 0)))
        out_shape.append(jax.ShapeDtypeStruct((heads, s, 1), F32))
        scratch_shapes.append(pltpu.VMEM((nk, 1, bk), F32))
    return pl.pallas_call(
        body, name=name, grid=(heads, nq), in_specs=in_specs, out_specs=out_specs, out_shape=out_shape,
        scratch_shapes=scratch_shapes, compiler_params=_params(("arbitrary", "arbitrary")))(*args)


def attention(q, k, v, cum, heads, scale, mode, tag):
    s = q.shape[0]
    _, bk = _attn_blocks(s)

    def layouts(cum):
        return cum.reshape(heads, s, 1), cum.reshape(heads, s // bk, 1, bk)

    @jax.custom_vjp
    def op(q, k, v, cum):
        cq, ck = layouts(cum)
        return _attn_fwd_call(q, k, v, cq, ck, heads, scale, mode, f"attn_{tag}")[0]

    def fwd(q, k, v, cum):
        cq, ck = layouts(cum)
        o, o32, lse = _attn_fwd_call(q, k, v, cq, ck, heads, scale, mode, f"attn_{tag}")
        return o, (q, k, v, cum, o32, lse)

    def bwd(saved, do):
        q, k, v, cum, o, lse = saved
        cq, ck = layouts(cum)
        res = _attn_bwd_call(q, k, v, cq, ck, o, lse, do, heads, scale, mode, f"attn_{tag}_bwd")
        dcum = res[3].reshape(heads, s) + res[4].reshape(heads, s) if mode == "fox" else jnp.zeros_like(cum)
        return res[0], res[1], res[2], dcum

    op.defvjp(fwd, bwd)
    return op(q, k, v, cum)


def _log_sigmoid(z):
    return jnp.minimum(z, 0.0) - jnp.log(1.0 + jnp.exp(-jnp.abs(z)))


def _tri_dot(x, tri):
    hi = x.astype(BF16)
    lo = (x - hi.astype(F32)).astype(BF16)
    return jnp.dot(hi, tri, preferred_element_type=F32) + jnp.dot(lo, tri, preferred_element_type=F32)


def _sb_call(q, k, v, keep_all, do, heads, scale, backward, name):
    s = q.shape[0]
    d = q.shape[1] // heads
    bq, bk = _sb_blocks(s)
    nq = s // bq

    def body(*refs):
        q_ref, k_ref, v_ref = refs[:3]
        qi = pl.program_id(1)
        qv = q_ref[...].astype(BF16)
        q_pos = qi * bq + lax.broadcasted_iota(jnp.int32, (bq, bk), 0)
        k_col = lax.broadcasted_iota(jnp.int32, (bq, bk), 1)
        r_i = lax.broadcasted_iota(jnp.int32, (bk, bk), 0)
        c_i = lax.broadcasted_iota(jnp.int32, (bk, bk), 1)
        hi = (qi * bq + bq - 1) // bk

        def logits(kj):
            rows = pl.ds(pl.multiple_of(kj * bk, bk), bk)
            kb = k_ref[rows, :].astype(BF16)
            z = lax.dot_general(qv, kb, (((1,), (1,)), ((), ())), preferred_element_type=F32) * scale
            past = (kj * bk + k_col) < q_pos
            ls = _log_sigmoid(z)
            lk = jnp.where(past, ls - z, 0.0)
            return rows, kb, z, past, ls, lk

        if not backward:
            o_ref, keep_ref = refs[3:5]
            after = jnp.where(r_i > c_i, 1.0, 0.0).astype(BF16)

            def step(it, carry):
                keep_right, acc = carry
                rows, _, _, past, ls, lk = logits(hi - it)
                a = jnp.where(past, jnp.exp(ls + _tri_dot(lk, after) + keep_right), 0.0)
                acc = acc + jnp.dot(a.astype(BF16), v_ref[rows, :].astype(BF16), preferred_element_type=F32)
                return keep_right + jnp.sum(lk, axis=1, keepdims=True), acc

            keep, acc = lax.fori_loop(0, hi + 1, step, (jnp.zeros((bq, 1), F32), jnp.zeros((bq, d), F32)))
            o_ref[...] = acc.astype(o_ref.dtype)
            keep_ref[...] = keep
            return

        keep_ref, do_ref, dq_ref, dk_ref, dv_ref, dk_acc, dv_acc = refs[3:]

        @pl.when(qi == 0)
        def _():
            dk_acc[...] = jnp.zeros_like(dk_acc)
            dv_acc[...] = jnp.zeros_like(dv_acc)

        dov = do_ref[...].astype(BF16)
        keep_total = keep_ref[...]
        upto = jnp.where(r_i <= c_i, 1.0, 0.0).astype(BF16)
        before = jnp.where(r_i < c_i, 1.0, 0.0).astype(BF16)

        def step(kj, carry):
            keep_left, g_left, dq = carry
            rows, kb, z, past, ls, lk = logits(kj)
            between = keep_total - (keep_left + _tri_dot(lk, upto))
            a = jnp.where(past, jnp.exp(ls + between), 0.0)
            vb = v_ref[rows, :].astype(BF16)
            da = lax.dot_general(dov, vb, (((1,), (1,)), ((), ())), preferred_element_type=F32)
            g = a * da
            g_before = g_left + _tri_dot(g, before)
            dz = jnp.where(past, g * jnp.exp(ls - z) - jnp.exp(ls) * g_before, 0.0)
            dzb = dz.astype(BF16)
            dk_acc[rows, :] += lax.dot_general(dzb, qv, (((0,), (0,)), ((), ())), preferred_element_type=F32)
            dv_acc[rows, :] += lax.dot_general(a.astype(BF16), dov, (((0,), (0,)), ((), ())),
                                               preferred_element_type=F32)
            dq = dq + jnp.dot(dzb, kb, preferred_element_type=F32)
            return (keep_left + jnp.sum(lk, axis=1, keepdims=True),
                    g_left + jnp.sum(g, axis=1, keepdims=True), dq)

        init = (jnp.zeros((bq, 1), F32), jnp.zeros((bq, 1), F32), jnp.zeros((bq, d), F32))
        _, _, dq = lax.fori_loop(0, hi + 1, step, init)
        dq_ref[...] = (dq * scale).astype(dq_ref.dtype)

        @pl.when(qi == nq - 1)
        def _():
            dk_ref[...] = (dk_acc[...] * scale).astype(dk_ref.dtype)
            dv_ref[...] = dv_acc[...].astype(dv_ref.dtype)

    blk_q = pl.BlockSpec((bq, d), lambda h, i: (i, h))
    blk_kv = pl.BlockSpec((s, d), lambda h, i: (0, h))
    blk_row = pl.BlockSpec((None, bq, 1), lambda h, i: (h, i, 0))
    if not backward:
        return pl.pallas_call(
            body, name=name, grid=(heads, nq), in_specs=[blk_q, blk_kv, blk_kv], out_specs=[blk_q, blk_row],
            out_shape=[jax.ShapeDtypeStruct(q.shape, BF16), jax.ShapeDtypeStruct((heads, s, 1), F32)],
            compiler_params=_params(("parallel", "arbitrary")))(q, k, v)
    return pl.pallas_call(
        body, name=name, grid=(heads, nq), in_specs=[blk_q, blk_kv, blk_kv, blk_row, blk_q],
        out_specs=[blk_q, blk_kv, blk_kv],
        out_shape=[jax.ShapeDtypeStruct(q.shape, q.dtype), jax.ShapeDtypeStruct(k.shape, k.dtype),
                   jax.ShapeDtypeStruct(v.shape, v.dtype)],
        scratch_shapes=[pltpu.VMEM((s, d), F32), pltpu.VMEM((s, d), F32)],
        compiler_params=_params(("arbitrary", "arbitrary")))(q, k, v, keep_all, do)


def stick_breaking(q, k, v, heads, scale, tag):
    @jax.custom_vjp
    def op(q, k, v):
        return _sb_call(q, k, v, None, None, heads, scale, False, f"sb_{tag}")[0]

    def fwd(q, k, v):
        o, keep_all = _sb_call(q, k, v, None, None, heads, scale, False, f"sb_{tag}")
        return o, (q, k, v, keep_all)

    def bwd(saved, do):
        q, k, v, keep_all = saved
        return tuple(_sb_call(q, k, v, keep_all, do, heads, scale, True, f"sb_{tag}_bwd"))

    op.defvjp(fwd, bwd)
    return op(q, k, v)


def _tri_dot3(x, tri):
    hi = x.astype(BF16)
    r1 = x - hi.astype(F32)
    mid = r1.astype(BF16)
    lo = (r1 - mid.astype(F32)).astype(BF16)
    return (jnp.dot(hi, tri, preferred_element_type=F32) + jnp.dot(mid, tri, preferred_element_type=F32)
            + jnp.dot(lo, tri, preferred_element_type=F32))


def _gate_call(f, bias_b, dcum, name):
    heads, r, _ = f.shape
    backward = dcum is not None

    def body(*refs):
        f_ref, b_ref = refs[:2]
        r_i = lax.broadcasted_iota(jnp.int32, (LANES, LANES), 0)
        c_i = lax.broadcasted_iota(jnp.int32, (LANES, LANES), 1)
        x = f_ref[...] + b_ref[...]
        if not backward:
            o_ref, ls_ref = refs[2:]
            ls_ref[...] = _log_sigmoid(x)
            o_ref[...] = _tri_dot3(ls_ref[...], jnp.where(r_i <= c_i, 1.0, 0.0).astype(BF16))

            def row(i, carry):
                o_ref[pl.ds(i, 1), :] = o_ref[pl.ds(i, 1), :] + carry
                return carry + jnp.sum(ls_ref[pl.ds(i, 1), :], axis=1, keepdims=True)

            lax.fori_loop(0, r, row, jnp.zeros((1, 1), F32))
            return

        dc_ref, df_ref, db_ref, acc_ref = refs[2:]
        acc_ref[...] = _tri_dot3(dc_ref[...], jnp.where(r_i >= c_i, 1.0, 0.0).astype(BF16))

        def row(it, carry):
            i = r - 1 - it
            acc_ref[pl.ds(i, 1), :] = acc_ref[pl.ds(i, 1), :] + carry
            return carry + jnp.sum(dc_ref[pl.ds(i, 1), :], axis=1, keepdims=True)

        lax.fori_loop(0, r, row, jnp.zeros((1, 1), F32))
        df = acc_ref[...] * jnp.exp(_log_sigmoid(-x))
        df_ref[...] = df
        lane = lax.broadcasted_iota(jnp.int32, (1, LANES), 1)
        db_ref[...] = jnp.where(lane == 0, jnp.sum(df), 0.0)

    blk = pl.BlockSpec((None, r, LANES), lambda h: (h, 0, 0))
    blk_b = pl.BlockSpec((None, 1, LANES), lambda h: (h, 0, 0))
    if not backward:
        return pl.pallas_call(
            body, name=name, grid=(heads,), in_specs=[blk, blk_b], out_specs=blk,
            out_shape=jax.ShapeDtypeStruct(f.shape, F32), scratch_shapes=[pltpu.VMEM((r, LANES), F32)],
            compiler_params=_params(("parallel",)))(f, bias_b)
    return pl.pallas_call(
        body, name=name, grid=(heads,), in_specs=[blk, blk_b, blk], out_specs=[blk, blk_b],
        out_shape=[jax.ShapeDtypeStruct(f.shape, F32), jax.ShapeDtypeStruct(bias_b.shape, F32)],
        scratch_shapes=[pltpu.VMEM((r, LANES), F32)],
        compiler_params=_params(("parallel",)))(f, bias_b, dcum)


def forget_gate_cumsum(f, bias_b, tag):
    @jax.custom_vjp
    def op(f, bias_b):
        return _gate_call(f, bias_b, None, f"gate_{tag}")

    def fwd(f, bias_b):
        return _gate_call(f, bias_b, None, f"gate_{tag}"), (f, bias_b)

    def bwd(saved, dcum):
        return tuple(_gate_call(saved[0], saved[1], dcum, f"gate_{tag}_bwd"))

    op.defvjp(fwd, bwd)
    return op(f, bias_b)


def _swiglu_call(g, u, dact, name):
    s, n = g.shape
    ts = _tile(s, 256, 128)
    backward = dact is not None

    def body(*refs):
        gv = refs[0][...].astype(F32)
        uv = refs[1][...].astype(F32)
        sig = 1.0 / (1.0 + jnp.exp(-gv))
        if not backward:
            refs[2][...] = (gv * sig * uv).astype(refs[2].dtype)
            return
        dv = refs[2][...].astype(F32)
        refs[3][...] = (dv * uv * sig * (1.0 + gv * (1.0 - sig))).astype(refs[3].dtype)
        refs[4][...] = (dv * gv * sig).astype(refs[4].dtype)

    blk = pl.BlockSpec((ts, n), lambda i: (i, 0))
    shape = jax.ShapeDtypeStruct((s, n), g.dtype)
    if not backward:
        return pl.pallas_call(body, name=name, grid=(s // ts,), in_specs=[blk, blk], out_specs=blk,
                              out_shape=shape, compiler_params=_params(("parallel",)))(g, u)
    return pl.pallas_call(body, name=name, grid=(s // ts,), in_specs=[blk, blk, blk], out_specs=[blk, blk],
                          out_shape=[shape, shape], compiler_params=_params(("parallel",)))(g, u, dact)


def swiglu(g, u, tag):
    @jax.custom_vjp
    def op(g, u):
        return _swiglu_call(g, u, None, f"swiglu_{tag}")

    def fwd(g, u):
        return _swiglu_call(g, u, None, f"swiglu_{tag}"), (g, u)

    def bwd(saved, dact):
        return tuple(_swiglu_call(saved[0], saved[1], dact, f"swiglu_{tag}_bwd"))

    op.defvjp(fwd, bwd)
    return op(g, u)


def final_norm_loss(x, gain, target):
    s, d = x.shape
    ts = _tile(s, 256, 128)

    def body(x_ref, g_ref, t_ref, sq_ref, dx_ref, dg_ref):
        @pl.when(pl.program_id(0) == 0)
        def _():
            sq_ref[...] = jnp.zeros_like(sq_ref)
            dg_ref[...] = jnp.zeros_like(dg_ref)

        xv = x_ref[...]
        r = lax.rsqrt(jnp.mean(xv * xv, axis=1, keepdims=True) + EPS)
        xhat = xv * r
        err = xhat * g_ref[...] - t_ref[...]
        sq_ref[...] += jnp.sum(err * err)
        dy = err * (1.0 / d)
        gdy = dy * g_ref[...]
        dx_ref[...] = r * (gdy - xhat * jnp.mean(gdy * xhat, axis=1, keepdims=True))
        dg_ref[...] += jnp.sum(dy * xhat, axis=0, keepdims=True)

    blk = pl.BlockSpec((ts, d), lambda i: (i, 0))
    row = pl.BlockSpec((1, d), lambda i: (0, 0))
    return pl.pallas_call(
        body, name="final_norm_loss", grid=(s // ts,), in_specs=[blk, row, blk],
        out_specs=[pl.BlockSpec((1, LANES), lambda i: (0, 0)), blk, row],
        out_shape=[jax.ShapeDtypeStruct((1, LANES), F32), jax.ShapeDtypeStruct((s, d), F32),
                   jax.ShapeDtypeStruct((1, d), F32)],
        compiler_params=_params(("arbitrary",)))(x, gain.reshape(1, d), target)


def adamw(w, g, m, v, name):
    rows, cols = w.shape
    tr = _tile(rows, 256, 8)
    c1 = 1.0 - ADAM_B1 ** ADAM_STEP
    c2 = 1.0 - ADAM_B2 ** ADAM_STEP

    def body(w_ref, g_ref, m_ref, v_ref, d_ref, nm_ref, nv_ref):
        gv = g_ref[...]
        m_new = ADAM_B1 * m_ref[...] + (1.0 - ADAM_B1) * gv
        v_new = ADAM_B2 * v_ref[...] + (1.0 - ADAM_B2) * (gv * gv)
        d_ref[...] = -ADAM_LR * ((m_new / c1) / (jnp.sqrt(v_new / c2) + ADAM_EPS) + ADAM_WD * w_ref[...])
        nm_ref[...] = m_new
        nv_ref[...] = v_new

    blk = pl.BlockSpec((tr, cols), lambda i: (i, 0))
    shape = jax.ShapeDtypeStruct((rows, cols), F32)
    return pl.pallas_call(body, name=name, grid=(rows // tr,), in_specs=[blk] * 4, out_specs=[blk] * 3,
                          out_shape=[shape] * 3, compiler_params=_params(("parallel",)))(w, g, m, v)


ANY = pl.BlockSpec(memory_space=pl.ANY)


def _position():
    return lax.axis_index("x"), lax.axis_index("y"), lax.axis_index("c")


HBM = pl.BlockSpec(memory_space=pltpu.HBM)
SEM = pl.BlockSpec(memory_space=pltpu.SEMAPHORE)
DATAFLOW = pltpu.SideEffectType.DATAFLOW_SIDE_EFFECTING


def _gather_copies(srcs, lands, send_sems, recv_sems):
    x, y, c = _position()
    chips = [(1 - x, y), (x, 1 - y), (1 - x, 1 - y)]
    return [pltpu.make_async_remote_copy(
        src_ref=srcs[a], dst_ref=lands[a].at[2 * x + y], send_sem=send_sems.at[3 * a + k], recv_sem=recv_sems.at[3 * a + k],
        device_id=(chips[k][0], chips[k][1], c), device_id_type=MESH) for a in range(len(srcs)) for k in range(3)]


def _scatter_copies(srcs, lands, send_sems, recv_sems):
    x, y, c = _position()
    out = []
    for a in range(len(srcs)):
        for k in range(1, 8):
            px, py, pc = (1 - x if k & 4 else x, 1 - y if k & 2 else y, 1 - c if k & 1 else c)
            out.append(pltpu.make_async_remote_copy(
                src_ref=srcs[a].at[2 * px + py, pc], dst_ref=lands[a].at[4 * x + 2 * y + c],
                send_sem=send_sems.at[7 * a + k - 1], recv_sem=recv_sems.at[7 * a + k - 1],
                device_id=(px, py, pc), device_id_type=MESH))
    return out


def _swap_copies(srcs, lands, send_sems, recv_sems):
    x, y, c = _position()
    return [pltpu.make_async_remote_copy(
        src_ref=srcs[a], dst_ref=lands[a].at[c], send_sem=send_sems.at[a], recv_sem=recv_sems.at[a],
        device_id=(x, y, 1 - c), device_id_type=MESH) for a in range(len(srcs))]


EXCHANGES = {"gather": (_gather_copies, 3), "scatter": (_scatter_copies, 7), "swap": (_swap_copies, 1)}


def start_exchange(kind, srcs, lands, after, name):
    copies, per_array = EXCHANGES[kind]
    n = len(srcs)

    def body(*refs):
        for cp in copies(refs[:n], refs[n:2 * n], refs[2 * n + 1], refs[2 * n + 2]):
            cp.start()
        refs[-1][...] = jnp.zeros_like(refs[-1])

    sems = pltpu.SemaphoreType.DMA((n * per_array,))
    hbm = [pltpu.HBM(a.shape, a.dtype) for a in list(srcs) + list(lands)]
    outs = pl.pallas_call(
        body, name=name, in_specs=[HBM] * (2 * n) + [ANY],
        out_shape=(sems, sems, *hbm, jax.ShapeDtypeStruct((8, LANES), F32)),
        out_specs=(SEM, SEM, *[HBM] * (2 * n), pl.BlockSpec(memory_space=pltpu.VMEM)),
        input_output_aliases={i: 2 + i for i in range(2 * n)},
        compiler_params=pltpu.CompilerParams(has_side_effects=DATAFLOW),
    )(*[pltpu.with_memory_space_constraint(a, pltpu.HBM) for a in list(srcs) + list(lands)], after)
    return (kind, n, outs[:-1]), outs[-1]


def finish_exchange(state, after, name):
    kind, n, (send_sems, recv_sems, *buffers) = state
    copies, _ = EXCHANGES[kind]

    def body(*refs):
        for cp in copies(refs[:n], refs[n:2 * n], refs[2 * n], refs[2 * n + 1]):
            cp.wait_send()
            cp.wait_recv()

    outs = pl.pallas_call(
        body, name=name, in_specs=[HBM] * (2 * n) + [SEM, SEM, ANY],
        out_shape=[pltpu.HBM(a.shape, a.dtype) for a in buffers], out_specs=[HBM] * (2 * n),
        input_output_aliases={i: i for i in range(2 * n)},
        compiler_params=pltpu.CompilerParams(has_side_effects=DATAFLOW),
    )(*buffers, send_sems, recv_sems, after)
    return outs[n:]


def _own_slot(land_shape, dtype, block, slot):
    start = (slot,) + (0,) * block.ndim
    return lax.dynamic_update_slice(lax.empty(land_shape, dtype), block[None], start)


def broadcast_all(block, name):
    r, lanes = block.shape

    def body(b_ref, out_ref, send_sems, recv_sems, local_sem):
        x, y, c = _position()
        me = 4 * x + 2 * y + c
        mine = pltpu.make_async_copy(b_ref, out_ref.at[me], local_sem)
        mine.start()

        def copy(k, to_me):
            px, py, pc = (1 - x if k & 4 else x, 1 - y if k & 2 else y, 1 - c if k & 1 else c)
            slot = 4 * px + 2 * py + pc if to_me else me
            return pltpu.make_async_remote_copy(
                src_ref=b_ref, dst_ref=out_ref.at[slot], send_sem=send_sems.at[k - 1],
                recv_sem=recv_sems.at[k - 1], device_id=(px, py, pc), device_id_type=MESH)

        sends = [copy(k, False) for k in range(1, 8)]
        for cp in sends:
            cp.start()
        for k in range(1, 8):
            copy(k, True).wait_recv()
        for cp in sends:
            cp.wait_send()
        mine.wait()

    return pl.pallas_call(
        body, name=name, in_specs=[ANY], out_specs=ANY,
        out_shape=jax.ShapeDtypeStruct((8, r, lanes), block.dtype),
        scratch_shapes=[pltpu.SemaphoreType.DMA((7,)), pltpu.SemaphoreType.DMA((7,)), pltpu.SemaphoreType.DMA])(block)


def sum_slots(stack, name):
    n, r, lanes = stack.shape
    tr = _tile(r, max(16, (1 << 22) // (n * lanes * stack.dtype.itemsize) // 16 * 16), 16)

    def body(s_ref, o_ref):
        acc = s_ref[0].astype(F32)
        for i in range(1, n):
            acc = acc + s_ref[i].astype(F32)
        o_ref[...] = acc

    return pl.pallas_call(
        body, name=name, grid=(r // tr,), in_specs=[pl.BlockSpec((n, tr, lanes), lambda i: (0, i, 0))],
        out_specs=pl.BlockSpec((tr, lanes), lambda i: (i, 0)), out_shape=jax.ShapeDtypeStruct((r, lanes), F32),
        compiler_params=_params(("parallel",)))(stack)


def _pack(arrays, dtype, row_align):
    flat = jnp.concatenate([a.reshape(-1).astype(dtype) for a in arrays])
    rows = -(-flat.shape[0] // LANES)
    rows = -(-rows // row_align) * row_align
    return jnp.pad(flat, (0, rows * LANES - flat.shape[0])).reshape(rows, LANES)


def _unpack(flat, shapes):
    flat = flat.reshape(flat.shape[:-2] + (-1,))
    out, at = [], 0
    for shp in shapes:
        n = math.prod(shp)
        out.append(flat[..., at:at + n].reshape(flat.shape[:-1] + tuple(shp)))
        at += n
    return out


def _pad_cols(w):
    return jnp.pad(w, ((0, 0), (0, -w.shape[1] % LANES)))


def split_columns(raw, width, widths, dtypes):
    wp = raw.shape[1] * width // sum(widths)
    jn = raw.shape[1] // wp

    def cut(raw):
        whole = jnp.concatenate([raw[:, j * wp:j * wp + width] for j in range(jn)], axis=1)
        out, at = [], 0
        for n, dt in zip(widths, dtypes):
            out.append(whole[:, at:at + n].astype(dt))
            at += n
        return tuple(out)

    @jax.custom_vjp
    def op(raw):
        return cut(raw)

    def fwd(raw):
        return cut(raw), None

    def bwd(_, cts):
        whole = jnp.concatenate([ct.astype(BF16) for ct in cts], axis=1)
        gap = jnp.zeros((whole.shape[0], wp - width), BF16)
        blocks = []
        for j in range(jn):
            blocks += [whole[:, j * width:(j + 1) * width], gap]
        return (jnp.concatenate(blocks, axis=1).astype(raw.dtype),)

    op.defvjp(fwd, bwd)
    return op(raw)


def _mixer_half(x, mats, smalls, tables, widths, heads, tag):
    s, d_model = x.shape
    gw = heads * HEAD_DIM
    w_in, w_uq, w_ukv, w_out = mats
    attn_norm, q_norm, kv_norm, f_bias, group_norm = smalls
    rope_full, rope_mla = tables
    no_cum = jnp.zeros((heads, s), F32)

    h = rms_norm(x, attn_norm, 1, BF16, f"attn_{tag}")
    raw = matmul(h, w_in, F32, f"in_{tag}")
    sizes = [Q_LORA, KV_LORA, QK_ROPE, 2 * gw, gw, gw, gw, gw, heads, gw, gw, gw]
    kinds = [F32, F32, F32, F32, BF16, BF16, BF16, BF16, F32, BF16, BF16, BF16]
    q_lat, kv_lat, k_rope, qk_b, v_b, q_c, k_c, v_c, f_logit, q_d, k_d, v_d = split_columns(
        raw, widths[0], sizes, kinds)

    q_all = matmul(rms_norm(q_lat, q_norm, 1, BF16, f"qlat_{tag}"), w_uq, BF16, f"uq_{tag}")
    kv_all = matmul(rms_norm(kv_lat, kv_norm, 1, BF16, f"kvlat_{tag}"), w_ukv, BF16, f"ukv_{tag}")
    q_a = rope(q_all, rope_mla, True, BF16, f"qpe_{tag}")
    k_pe = rope(_pad_cols(k_rope), rope_mla, False, BF16, f"kpe_{tag}")
    kv4 = kv_all.reshape(s, heads, 2, LANES)
    k_a = jnp.stack([kv4[:, :, 0], jnp.broadcast_to(k_pe[:, None, :], (s, heads, LANES))], axis=2).reshape(s, 2 * gw)
    out_a = attention(q_a, k_a, kv4[:, :, 1].reshape(s, gw), no_cum, heads, (QK_NOPE + QK_ROPE) ** -0.5,
                      "causal", f"a_{tag}")

    qk_b = rope(qk_b, rope_full, False, BF16, f"qkb_{tag}")
    out_b = attention(qk_b[:, :gw], qk_b[:, gw:], v_b, no_cum, heads, HEAD_DIM ** -0.5, "dilated", f"b_{tag}")

    f_rows = f_logit.T.reshape(heads, s // LANES, LANES)
    bias_b = jnp.broadcast_to(f_bias[:, None, None], (heads, 1, LANES))
    cum = forget_gate_cumsum(f_rows, bias_b, tag).reshape(heads, s)
    out_c = attention(q_c, k_c, v_c, cum, heads, HEAD_DIM ** -0.5, "fox", f"c_{tag}")

    out_d = stick_breaking(q_d, k_d, v_d, heads, HEAD_DIM ** -0.5, f"d_{tag}")

    groups = jnp.concatenate([out_a, out_b, out_c, out_d], axis=1)
    return matmul_res(rms_norm(groups, group_norm, 4, BF16, f"group_{tag}"), w_out.reshape(-1, d_model), x, f"out_{tag}")


def _ffn_half(x, mats, smalls, tag):
    d_model = x.shape[1]
    w_gate, w_up, w_down = mats
    h2 = rms_norm(x, smalls[0], 1, BF16, f"ffn_{tag}")
    act = swiglu(matmul(h2, w_gate, BF16, f"gate_{tag}"), matmul(h2, w_up, BF16, f"up_{tag}"), tag)
    return matmul_res(act, w_down.reshape(-1, d_model), x, f"down_{tag}")


def kernel(x, attn_norm, w_in, mla_q_norm, w_uq, mla_kv_norm, w_ukv, fox_forget_bias, group_norm, w_out, ffn_norm, w_gate, w_up, w_down, final_norm, loss_target, m_attn_norm, m_w_in, m_mla_q_norm, m_w_uq, m_mla_kv_norm, m_w_ukv, m_fox_forget_bias, m_group_norm, m_w_out, m_ffn_norm, m_w_gate, m_w_up, m_w_down, m_final_norm, v_attn_norm, v_w_in, v_mla_q_norm, v_w_uq, v_mla_kv_norm, v_w_ukv, v_fox_forget_bias, v_group_norm, v_w_out, v_ffn_norm, v_w_gate, v_w_up, v_w_down, v_final_norm):
    depth = w_in.shape[0]
    _, s, d_model = x.shape
    heads = d_model // 4 // HEAD_DIM
    big = [w_in, w_uq, w_ukv, w_out, w_gate, w_up, w_down]
    big_m = [m_w_in, m_w_uq, m_w_ukv, m_w_out, m_w_gate, m_w_up, m_w_down]
    big_v = [v_w_in, v_w_uq, v_w_ukv, v_w_out, v_w_gate, v_w_up, v_w_down]
    small = [attn_norm, mla_q_norm, mla_kv_norm, fox_forget_bias, group_norm, ffn_norm]
    small_m = [m_attn_norm, m_mla_q_norm, m_mla_kv_norm, m_fox_forget_bias, m_group_norm, m_ffn_norm, m_final_norm]
    small_v = [v_attn_norm, v_mla_q_norm, v_mla_kv_norm, v_fox_forget_bias, v_group_norm, v_ffn_norm, v_final_norm]
    widths = [w.shape[2] for w in big]
    tables = (rope_tables(s, HEAD_DIM), rope_tables(s, QK_ROPE))

    px, py, pc = _position()

    n_units = 2 * depth
    mixer_w, mixer_s = 4, 5

    def unit_arrays(u, arrays, split):
        l, ffn = divmod(u, 2)
        return [a[l] for a in (arrays[split:] if ffn else arrays[:split])]

    def unit_fn(u):
        tag = str(u // 2)
        if u % 2:
            return lambda g, sm, xx: _ffn_half(xx, g, sm, tag)
        return lambda g, sm, xx: _mixer_half(xx, g, sm, tables, widths, heads, tag)

    def start_gather(u, after):
        shards = [_pad_cols(w).astype(BF16) for w in unit_arrays(u, big, mixer_w)]
        lands = [_own_slot((4,) + sh.shape, BF16, sh, 2 * px + py) for sh in shards]
        return start_exchange("gather", shards, lands, after, f"gather_start_{u}")

    state, token = start_gather(0, x)
    gathered = finish_exchange(state, token, "gather_wait_0")
    x_l, vjps = x[0], []
    for u in range(n_units):
        small_u = unit_arrays(u, small, mixer_s)
        if u + 1 < n_units:
            state, token = start_gather(u + 1, gathered[0])
            small_u[0] = small_u[0] + token[0, 0]
        x_l, vjp_u = jax.vjp(unit_fn(u), gathered, small_u, x_l)
        vjps.append(vjp_u)
        if u + 1 < n_units:
            gathered = finish_exchange(state, x_l, f"gather_wait_{u + 1}")

    sq, dx, d_final = final_norm_loss(x_l, final_norm, loss_target[0])
    loss = 0.5 / d_model * lax.psum(sq[0, 0], ("x", "y", "c"))

    d_small_units = [None] * n_units
    grad_units = [None] * n_units
    scatters, swaps = {}, {}

    def finish_scatter(u, after):
        recv = finish_exchange(scatters.pop(u), after, f"scatter_wait_{u}")
        halves = [sum_slots(r, f"sum_{u}_{i}") for i, r in enumerate(recv)]
        lands = [_own_slot((2,) + h.shape, F32, h, pc) for h in halves]
        swaps[u], token = start_exchange("swap", halves, lands, after, f"swap_start_{u}")
        return token

    def finish_swap(u, after):
        full = finish_exchange(swaps.pop(u), after, f"swap_wait_{u}")
        unit_widths = widths[mixer_w:] if u % 2 else widths[:mixer_w]
        grad_units[u] = [f.reshape(-1, f.shape[2])[:, :wd] for f, wd in zip(full, unit_widths)]

    for u in reversed(range(n_units)):
        d_g, d_small_units[u], dx = vjps[u](dx)
        parts = [g.reshape(4, 2, g.shape[1] // 2, g.shape[2]) for g in d_g]
        lands = [_own_slot((8,) + p.shape[2:], BF16, p[2 * px + py, pc], 4 * px + 2 * py + pc) for p in parts]
        scatters[u], token = start_exchange("scatter", parts, lands, dx, f"scatter_start_{u}")
        tokens = token[0, 0]
        if u + 2 < n_units:
            finish_swap(u + 2, dx)
        if u + 1 < n_units:
            tokens = tokens + finish_scatter(u + 1, dx)[0, 0]
        dx = dx + tokens
    finish_swap(1, dx)
    token = finish_scatter(0, dx)
    finish_swap(0, token)
    big_g = [jnp.stack([(grad_units[2 * l] + grad_units[2 * l + 1])[i] for l in range(depth)])
             for i in range(len(big))]
    d_small = [jnp.stack([(list(d_small_units[2 * l]) + list(d_small_units[2 * l + 1]))[i] for l in range(depth)])
               for i in range(len(small))]

    small_all = small + [final_norm]
    packed = _pack(d_small + [d_final], F32, 8)
    small_sum = sum_slots(broadcast_all(packed, "small_gather"), "small_sum")
    small_g = _unpack(small_sum, [p.shape for p in small_all])

    def update(w, g, m, v, name):
        shp = w.shape
        two_d = (-1, shp[-1])
        return [o.reshape(shp) for o in adamw(w.reshape(two_d), g.reshape(two_d), m.reshape(two_d),
                                              v.reshape(two_d), name)]

    big_u = [update(w, g, m, v, f"adamw_big{i}") for i, (w, g, m, v) in enumerate(zip(big, big_g, big_m, big_v))]
    small_u = adamw(_pack(small_all, F32, 8), small_sum, _pack(small_m, F32, 8), _pack(small_v, F32, 8), "adamw_small")
    small_u = [_unpack(u, [p.shape for p in small_all]) for u in small_u]

    def ordered(bigs, smalls):
        a_n, q_n, kv_n, f_b, g_n, f_n, fin = smalls
        wi, uq, ukv, wo, wg, wu, wd = bigs
        return [a_n, wi, q_n, uq, kv_n, ukv, f_b, g_n, wo, f_n, wg, wu, wd, fin]

    grads = ordered(big_g, small_g)
    deltas = ordered([u[0] for u in big_u], small_u[0])
    new_m = ordered([u[1] for u in big_u], small_u[1])
    new_v = ordered([u[2] for u in big_u], small_u[2])
    return (loss, dx[None], *grads, *deltas, *new_m, *new_v)
```

```python
import functools
import math

import jax
import jax.numpy as jnp
from jax import lax
from jax.experimental import pallas as pl
from jax.experimental.pallas import tpu as pltpu

F32 = jnp.float32
BF16 = jnp.bfloat16
MESH = pl.DeviceIdType.MESH

HEAD_DIM = 128
Q_LORA = 512
KV_LORA = 512
QK_NOPE = 128
QK_ROPE = 64
DILATED_PAIRS = ((128, 1), (512, 4), (2048, 16))
MAX_WINDOW = max(w for w, _ in DILATED_PAIRS)
ROPE_THETA = 10000.0
EPS = 1e-6
NEG_INF = -1e30
ADAM_LR = 0.001
ADAM_B1 = 0.9
ADAM_B2 = 0.999
ADAM_EPS = 1e-08
ADAM_WD = 0.01
ADAM_STEP = 10

LANES = 128
VMEM_LIMIT_BYTES = 56 * 1024 * 1024


def _params(semantics=None):
    return pltpu.CompilerParams(dimension_semantics=semantics, vmem_limit_bytes=VMEM_LIMIT_BYTES)


MATMUL_VMEM_BUDGET = 40 * 1024 * 1024
MAX_TILE_COLS = 1536


def _fit_tile(n, cap, estimate):
    t = _tile(n, cap, LANES)
    while estimate(t) > MATMUL_VMEM_BUDGET and t > LANES:
        t = _tile(n, t - LANES, LANES)
    return t


def _tile(n, cap, align):
    if n <= cap:
        return n
    t = cap - cap % align
    while t >= align:
        if n % t == 0:
            return t
        t -= align
    raise ValueError(f"no tile for {n} under {cap}")


def _col_blocks(b):
    return (b.shape[0], b.shape[2]) if b.ndim == 3 else (1, b.shape[1])


def _mm_nn(a, b, res, out_dtype, name):
    m, k = a.shape
    jn, np_ = _col_blocks(b)
    n = jn * np_
    tm = _tile(m, 1024, 128)
    res_bytes = 0 if res is None else res.dtype.itemsize
    tn = _fit_tile(np_, MAX_TILE_COLS, lambda t: 2 * (tm * k * a.dtype.itemsize + k * t * b.dtype.itemsize
                                                       + tm * t * (jnp.dtype(out_dtype).itemsize + res_bytes)))
    nb = np_ // tn

    def body(*refs):
        a_ref, b_ref = refs[0], refs[1]
        o_ref = refs[-1]
        acc = jnp.dot(a_ref[...].astype(BF16), b_ref[...].astype(BF16), preferred_element_type=F32)
        if res is not None:
            acc = acc + refs[2][...].astype(F32)
        o_ref[...] = acc.astype(o_ref.dtype)

    if b.ndim == 3:
        b_spec = pl.BlockSpec((None, k, tn), lambda i, j: (j // nb, 0, j % nb))
    else:
        b_spec = pl.BlockSpec((k, tn), lambda i, j: (0, j))
    in_specs = [pl.BlockSpec((tm, k), lambda i, j: (i, 0)), b_spec]
    args = [a, b]
    if res is not None:
        in_specs.append(pl.BlockSpec((tm, tn), lambda i, j: (i, j)))
        args.append(res)
    return pl.pallas_call(
        body, name=name, grid=(m // tm, n // tn), in_specs=in_specs,
        out_specs=pl.BlockSpec((tm, tn), lambda i, j: (i, j)),
        out_shape=jax.ShapeDtypeStruct((m, n), out_dtype),
        compiler_params=_params(("parallel", "arbitrary")))(*args)


def _mm_nt(a, b, out_dtype, name):
    m, n = a.shape
    jn, np_ = _col_blocks(b)
    k = b.shape[-2]
    tm = _tile(m, 1024, 128)
    tk = _fit_tile(k, 1024, lambda t: 2 * (tm * n * a.dtype.itemsize + t * n * b.dtype.itemsize
                                            + tm * t * jnp.dtype(out_dtype).itemsize))

    def body(a_ref, b_ref, o_ref):
        acc = None
        for j in range(jn):
            bj = b_ref[j] if b.ndim == 3 else b_ref[...]
            part = lax.dot_general(a_ref[:, j * np_:(j + 1) * np_].astype(BF16), bj.astype(BF16),
                                   (((1,), (1,)), ((), ())), preferred_element_type=F32)
            acc = part if acc is None else acc + part
        o_ref[...] = acc.astype(o_ref.dtype)

    if b.ndim == 3:
        b_spec = pl.BlockSpec((jn, tk, np_), lambda i, j: (0, j, 0))
    else:
        b_spec = pl.BlockSpec((tk, n), lambda i, j: (j, 0))
    return pl.pallas_call(
        body, name=name, grid=(m // tm, k // tk),
        in_specs=[pl.BlockSpec((tm, n), lambda i, j: (i, 0)), b_spec],
        out_specs=pl.BlockSpec((tm, tk), lambda i, j: (i, j)),
        out_shape=jax.ShapeDtypeStruct((m, k), out_dtype),
        compiler_params=_params(("parallel", "arbitrary")))(a, b)


def _mm_tn(a, b, like, name):
    m, k = a.shape
    _, n = b.shape
    jn, np_ = _col_blocks(like)
    tk = _tile(k, 512, LANES)
    tn = _fit_tile(np_, MAX_TILE_COLS, lambda t: 2 * (m * tk * a.dtype.itemsize + m * t * b.dtype.itemsize
                                                       + tk * t * like.dtype.itemsize))
    nb = np_ // tn

    def body(a_ref, b_ref, o_ref):
        acc = lax.dot_general(a_ref[...].astype(BF16), b_ref[...].astype(BF16),
                              (((0,), (0,)), ((), ())), preferred_element_type=F32)
        o_ref[...] = acc.astype(o_ref.dtype)

    if like.ndim == 3:
        out_spec = pl.BlockSpec((None, tk, tn), lambda i, j: (j // nb, i, j % nb))
    else:
        out_spec = pl.BlockSpec((tk, tn), lambda i, j: (i, j))
    return pl.pallas_call(
        body, name=name, grid=(k // tk, n // tn),
        in_specs=[pl.BlockSpec((m, tk), lambda i, j: (0, i)), pl.BlockSpec((m, tn), lambda i, j: (0, j))],
        out_specs=out_spec, out_shape=jax.ShapeDtypeStruct(like.shape, like.dtype),
        compiler_params=_params(("parallel", "arbitrary")))(a, b)


def _matmul_op(out_dtype, with_res, tag):
    def fwd_only(a, w, res):
        return _mm_nn(a, w, res if with_res else None, out_dtype, f"mm_{tag}")

    @jax.custom_vjp
    def op(a, w, res):
        return fwd_only(a, w, res)

    def fwd(a, w, res):
        return fwd_only(a, w, res), (a, w, res)

    def bwd(saved, dc):
        a, w, res = saved
        if not with_res and dc.dtype != BF16:
            dc = dc.astype(BF16)
        da = _mm_nt(dc, w, a.dtype, f"mm_{tag}_da")
        dw = _mm_tn(a, dc, w, f"mm_{tag}_dw")
        dres = dc.astype(res.dtype) if with_res else jnp.zeros_like(res)
        return da, dw, dres

    op.defvjp(fwd, bwd)
    return op


def matmul(a, w, out_dtype, tag):
    return _matmul_op(out_dtype, False, tag)(a, w, jnp.zeros((), F32))


def matmul_res(a, w, res, tag):
    return _matmul_op(res.dtype, True, tag)(a, w, res)


def _rms_fwd_call(x, gain, groups, out_dtype, name):
    s, w = x.shape
    gw = w // groups
    ts = _tile(s, 512, 128)

    def body(x_ref, g_ref, o_ref):
        for g in range(groups):
            cols = slice(g * gw, (g + 1) * gw)
            xv = x_ref[:, cols].astype(F32)
            r = lax.rsqrt(jnp.mean(xv * xv, axis=1, keepdims=True) + EPS)
            o_ref[:, cols] = (xv * r * g_ref[:, cols]).astype(o_ref.dtype)

    return pl.pallas_call(
        body, name=name, grid=(s // ts,),
        in_specs=[pl.BlockSpec((ts, w), lambda i: (i, 0)), pl.BlockSpec((1, w), lambda i: (0, 0))],
        out_specs=pl.BlockSpec((ts, w), lambda i: (i, 0)),
        out_shape=jax.ShapeDtypeStruct((s, w), out_dtype),
        compiler_params=_params(("parallel",)))(x, gain.reshape(1, w))


def _rms_bwd_call(x, gain, dy, groups, name):
    s, w = x.shape
    gw = w // groups
    ts = _tile(s, 512, 128)

    def body(x_ref, g_ref, dy_ref, dx_ref, dg_ref):
        @pl.when(pl.program_id(0) == 0)
        def _():
            dg_ref[...] = jnp.zeros_like(dg_ref)

        for g in range(groups):
            cols = slice(g * gw, (g + 1) * gw)
            xv = x_ref[:, cols].astype(F32)
            dyv = dy_ref[:, cols].astype(F32)
            r = lax.rsqrt(jnp.mean(xv * xv, axis=1, keepdims=True) + EPS)
            xhat = xv * r
            gdy = dyv * g_ref[:, cols]
            dx = r * (gdy - xhat * jnp.mean(gdy * xhat, axis=1, keepdims=True))
            dx_ref[:, cols] = dx.astype(dx_ref.dtype)
            dg_ref[:, cols] += jnp.sum(dyv * xhat, axis=0, keepdims=True)

    return pl.pallas_call(
        body, name=name, grid=(s // ts,),
        in_specs=[pl.BlockSpec((ts, w), lambda i: (i, 0)), pl.BlockSpec((1, w), lambda i: (0, 0)),
                  pl.BlockSpec((ts, w), lambda i: (i, 0))],
        out_specs=[pl.BlockSpec((ts, w), lambda i: (i, 0)), pl.BlockSpec((1, w), lambda i: (0, 0))],
        out_shape=[jax.ShapeDtypeStruct((s, w), x.dtype), jax.ShapeDtypeStruct((1, w), F32)],
        compiler_params=_params(("arbitrary",)))(x, gain.reshape(1, w), dy)


def rms_norm(x, gain, groups, out_dtype, tag):
    @jax.custom_vjp
    def op(x, gain):
        return _rms_fwd_call(x, gain, groups, out_dtype, f"rms_{tag}")

    def fwd(x, gain):
        return _rms_fwd_call(x, gain, groups, out_dtype, f"rms_{tag}"), (x, gain)

    def bwd(saved, dy):
        x, gain = saved
        dx, dg = _rms_bwd_call(x, gain, dy, groups, f"rms_{tag}_bwd")
        return dx, dg.reshape(gain.shape)

    op.defvjp(fwd, bwd)
    return op(x, gain)


def _rope_call(x, tables, odd_chunks_only, transpose, out_dtype, name):
    s, w = x.shape
    ts = _tile(s, 512, 128)
    cos_t, sin_lo, sin_hi, half = tables
    one_roll = 2 * half == LANES

    def body(x_ref, c_ref, lo_ref, hi_ref, o_ref):
        c, s_lo, s_hi = c_ref[...], lo_ref[...], hi_ref[...]
        for j in range(w // LANES):
            cols = slice(j * LANES, (j + 1) * LANES)
            if odd_chunks_only and j % 2 == 0:
                o_ref[:, cols] = x_ref[:, cols].astype(o_ref.dtype)
                continue
            xv = x_ref[:, cols].astype(F32)
            if one_roll:
                y = xv * c + (pltpu.roll(xv * (s_lo + s_hi), half, 1) if transpose
                              else pltpu.roll(xv, half, 1) * (s_lo + s_hi))
            elif transpose:
                y = xv * c + pltpu.roll(xv * s_lo, LANES - half, 1) + pltpu.roll(xv * s_hi, half, 1)
            else:
                y = xv * c + pltpu.roll(xv, half, 1) * s_lo + pltpu.roll(xv, LANES - half, 1) * s_hi
            o_ref[:, cols] = y.astype(o_ref.dtype)

    row = pl.BlockSpec((ts, LANES), lambda i: (i, 0))
    return pl.pallas_call(
        body, name=name, grid=(s // ts,),
        in_specs=[pl.BlockSpec((ts, w), lambda i: (i, 0)), row, row, row],
        out_specs=pl.BlockSpec((ts, w), lambda i: (i, 0)),
        out_shape=jax.ShapeDtypeStruct((s, w), out_dtype),
        compiler_params=_params(("parallel",)))(x, cos_t, sin_lo, sin_hi)


def rope(x, tables, odd_chunks_only, out_dtype, tag):
    @jax.custom_vjp
    def op(x):
        return _rope_call(x, tables, odd_chunks_only, False, out_dtype, f"rope_{tag}")

    def fwd(x):
        return _rope_call(x, tables, odd_chunks_only, False, out_dtype, f"rope_{tag}"), None

    def bwd(_, dy):
        return (_rope_call(dy, tables, odd_chunks_only, True, x.dtype, f"rope_{tag}_bwd"),)

    op.defvjp(fwd, bwd)
    return op(x)


def rope_tables(seq, dim):
    half = dim // 2
    pos = jnp.arange(seq, dtype=F32)
    inv_freq = ROPE_THETA ** (-jnp.arange(0, dim, 2, dtype=F32) / dim)
    ang = pos[:, None] * inv_freq[None, :]
    cos, sin = jnp.cos(ang), jnp.sin(ang)
    zero = jnp.zeros_like(sin)
    rest = jnp.zeros((seq, LANES - dim), F32)
    return (jnp.concatenate([cos, cos, rest], axis=1), jnp.concatenate([zero, sin, rest], axis=1),
            jnp.concatenate([-sin, zero, rest], axis=1), half)


def _attn_blocks(s):
    return min(256, s), min(512, s)


def _sb_blocks(s):
    b = min(256, s)
    return b, b


def _score_mask(delta, mode):
    if mode != "dilated":
        return None, delta >= 0
    mult = jnp.zeros(delta.shape, F32)
    for window, dilation in DILATED_PAIRS:
        hit = delta <= window
        if dilation > 1:
            hit = hit & ((delta & (dilation - 1)) == 0)
        mult = mult + jnp.where(hit, 1.0, 0.0)
    bias = jnp.where(mult > 2.5, math.log(3.0), jnp.where(mult > 1.5, math.log(2.0), 0.0))
    return bias, (delta >= 0) & (mult > 0.5)


def _kv_range(qi, bq, bk, mode):
    hi = (qi * bq + bq - 1) // bk
    if mode == "dilated":
        lo = jnp.maximum(qi * bq - MAX_WINDOW, 0) // bk
        return lo, lo, hi
    return 0, (qi * bq + 1) // bk, hi


def _attn_fwd_call(q, k, v, cq, ck, heads, scale, mode, name):
    s = q.shape[0]
    dk, dv = q.shape[1] // heads, v.shape[1] // heads
    bq, bk = _attn_blocks(s)
    nq, nk = s // bq, s // bk
    fox = mode == "fox"

    def body(*refs):
        q_ref, k_ref, v_ref = refs[:3]
        o_ref, o32_ref, lse_ref = refs[-3:]
        qi = pl.program_id(1)
        qv = q_ref[...].astype(BF16)
        row_col = (lax.broadcasted_iota(jnp.int32, (bq, bk), 0) - lax.broadcasted_iota(jnp.int32, (bq, bk), 1))
        lo, full, hi = _kv_range(qi, bq, bk, mode)

        def step(masked, kj, carry):
            m, l, acc = carry
            rows = pl.ds(pl.multiple_of(kj * bk, bk), bk)
            kb = k_ref[rows, :].astype(BF16)
            vb = v_ref[rows, :].astype(BF16)
            sc = lax.dot_general(qv, kb, (((1,), (1,)), ((), ())), preferred_element_type=F32) * scale
            if fox:
                sc = sc + refs[3][...] - refs[4][kj]
            if masked:
                bias, mask = _score_mask(row_col + (qi * bq - kj * bk), mode)
                if bias is not None:
                    sc = sc + bias
                sc = jnp.where(mask, sc, NEG_INF)
            m_new = jnp.maximum(m, jnp.max(sc, axis=1, keepdims=True))
            alpha = jnp.exp(m - m_new)
            p = jnp.exp(sc - m_new)
            l = l * alpha + jnp.sum(p, axis=1, keepdims=True)
            acc = acc * alpha + jnp.dot(p.astype(BF16), vb, preferred_element_type=F32)
            return m_new, l, acc

        carry = (jnp.full((bq, 1), NEG_INF, F32), jnp.zeros((bq, 1), F32), jnp.zeros((bq, dv), F32))
        carry = lax.fori_loop(lo, full, functools.partial(step, False), carry)
        m, l, acc = lax.fori_loop(full, hi + 1, functools.partial(step, True), carry)
        out = acc / l
        o_ref[...] = out.astype(o_ref.dtype)
        o32_ref[...] = out
        lse_ref[...] = m + jnp.log(l)

    in_specs = [pl.BlockSpec((bq, dk), lambda h, i: (i, h)), pl.BlockSpec((s, dk), lambda h, i: (0, h)),
                pl.BlockSpec((s, dv), lambda h, i: (0, h))]
    args = [q, k, v]
    if fox:
        in_specs += [pl.BlockSpec((None, bq, 1), lambda h, i: (h, i, 0)),
                     pl.BlockSpec((None, nk, 1, bk), lambda h, i: (h, 0, 0, 0))]
        args += [cq, ck]
    return pl.pallas_call(
        body, name=name, grid=(heads, nq), in_specs=in_specs,
        out_specs=[pl.BlockSpec((bq, dv), lambda h, i: (i, h)), pl.BlockSpec((bq, dv), lambda h, i: (i, h)),
                   pl.BlockSpec((None, bq, 1), lambda h, i: (h, i, 0))],
        out_shape=[jax.ShapeDtypeStruct((s, heads * dv), BF16), jax.ShapeDtypeStruct((s, heads * dv), F32),
                   jax.ShapeDtypeStruct((heads, s, 1), F32)],
        compiler_params=_params(("parallel", "arbitrary")))(*args)


def _attn_bwd_call(q, k, v, cq, ck, o, lse, do, heads, scale, mode, name):
    s = q.shape[0]
    dk, dv = q.shape[1] // heads, v.shape[1] // heads
    bq, bk = _attn_blocks(s)
    nq, nk = s // bq, s // bk
    fox = mode == "fox"
    n_in = 8 if fox else 6

    def body(*refs):
        q_ref, k_ref, v_ref = refs[:3]
        o_ref, lse_ref, do_ref = refs[n_in - 3:n_in]
        outs = refs[n_in:]
        dq_ref, dk_ref, dv_ref = outs[:3]
        scratch = outs[5:] if fox else outs[3:]
        dk_acc, dv_acc = scratch[:2]
        qi = pl.program_id(1)

        @pl.when(qi == 0)
        def _():
            dk_acc[...] = jnp.zeros_like(dk_acc)
            dv_acc[...] = jnp.zeros_like(dv_acc)
            if fox:
                scratch[2][...] = jnp.zeros_like(scratch[2])

        qv = q_ref[...].astype(BF16)
        dov = do_ref[...].astype(BF16)
        delta_o = jnp.sum(dov.astype(F32) * o_ref[...], axis=1, keepdims=True)
        lse = lse_ref[...]
        row_col = (lax.broadcasted_iota(jnp.int32, (bq, bk), 0) - lax.broadcasted_iota(jnp.int32, (bq, bk), 1))
        lo, full, hi = _kv_range(qi, bq, bk, mode)

        def step(masked, kj, carry):
            dq, dcq = carry
            rows = pl.ds(pl.multiple_of(kj * bk, bk), bk)
            kb = k_ref[rows, :].astype(BF16)
            vb = v_ref[rows, :].astype(BF16)
            sc = lax.dot_general(qv, kb, (((1,), (1,)), ((), ())), preferred_element_type=F32) * scale
            if fox:
                sc = sc + refs[3][...] - refs[4][kj]
            if masked:
                bias, mask = _score_mask(row_col + (qi * bq - kj * bk), mode)
                if bias is not None:
                    sc = sc + bias
                sc = jnp.where(mask, sc, NEG_INF)
            p = jnp.exp(sc - lse)
            dv_acc[rows, :] += lax.dot_general(p.astype(BF16), dov, (((0,), (0,)), ((), ())),
                                               preferred_element_type=F32)
            dp = lax.dot_general(dov, vb, (((1,), (1,)), ((), ())), preferred_element_type=F32)
            ds = p * (dp - delta_o)
            dsb = ds.astype(BF16)
            dk_acc[rows, :] += lax.dot_general(dsb, qv, (((0,), (0,)), ((), ())), preferred_element_type=F32)
            if fox:
                scratch[2][kj] -= jnp.sum(ds, axis=0, keepdims=True)
                dcq = dcq + jnp.sum(ds, axis=1, keepdims=True)
            return dq + jnp.dot(dsb, kb, preferred_element_type=F32), dcq

        carry = (jnp.zeros((bq, dk), F32), jnp.zeros((bq, 1), F32))
        carry = lax.fori_loop(lo, full, functools.partial(step, False), carry)
        dq, dcq = lax.fori_loop(full, hi + 1, functools.partial(step, True), carry)
        dq_ref[...] = (dq * scale).astype(dq_ref.dtype)
        if fox:
            outs[4][...] = dcq

        @pl.when(qi == nq - 1)
        def _():
            dk_ref[...] = (dk_acc[...] * scale).astype(dk_ref.dtype)
            dv_ref[...] = dv_acc[...].astype(dv_ref.dtype)
            if fox:
                outs[3][...] = scratch[2][...]

    in_specs = [pl.BlockSpec((bq, dk), lambda h, i: (i, h)), pl.BlockSpec((s, dk), lambda h, i: (0, h)),
                pl.BlockSpec((s, dv), lambda h, i: (0, h))]
    args = [q, k, v]
    if fox:
        in_specs += [pl.BlockSpec((None, bq, 1), lambda h, i: (h, i, 0)),
                     pl.BlockSpec((None, nk, 1, bk), lambda h, i: (h, 0, 0, 0))]
        args += [cq, ck]
    in_specs += [pl.BlockSpec((bq, dv), lambda h, i: (i, h)), pl.BlockSpec((None, bq, 1), lambda h, i: (h, i, 0)),
                 pl.BlockSpec((bq, dv), lambda h, i: (i, h))]
    args += [o, lse, do]
    out_specs = [pl.BlockSpec((bq, dk), lambda h, i: (i, h)), pl.BlockSpec((s, dk), lambda h, i: (0, h)),
                 pl.BlockSpec((s, dv), lambda h, i: (0, h))]
    out_shape = [jax.ShapeDtypeStruct(q.shape, q.dtype), jax.ShapeDtypeStruct(k.shape, k.dtype),
                 jax.ShapeDtypeStruct(v.shape, v.dtype)]
    scratch_shapes = [pltpu.VMEM((s, dk), F32), pltpu.VMEM((s, dv), F32)]
    if fox:
        out_specs.append(pl.BlockSpec((None, nk, 1, bk), lambda h, i: (h, 0, 0, 0)))
        out_shape.append(jax.ShapeDtypeStruct((heads, nk, 1, bk), F32))
        out_specs.append(pl.BlockSpec((None, bq, 1), lambda h, i: (h, i, 0)))
        out_shape.append(jax.ShapeDtypeStruct((heads, s, 1), F32))
        scratch_shapes.append(pltpu.VMEM((nk, 1, bk), F32))
    return pl.pallas_call(
        body, name=name, grid=(heads, nq), in_specs=in_specs, out_specs=out_specs, out_shape=out_shape,
        scratch_shapes=scratch_shapes, compiler_params=_params(("arbitrary", "arbitrary")))(*args)


def attention(q, k, v, cum, heads, scale, mode, tag):
    s = q.shape[0]
    _, bk = _attn_blocks(s)

    def layouts(cum):
        return cum.reshape(heads, s, 1), cum.reshape(heads, s // bk, 1, bk)

    @jax.custom_vjp
    def op(q, k, v, cum):
        cq, ck = layouts(cum)
        return _attn_fwd_call(q, k, v, cq, ck, heads, scale, mode, f"attn_{tag}")[0]

    def fwd(q, k, v, cum):
        cq, ck = layouts(cum)
        o, o32, lse = _attn_fwd_call(q, k, v, cq, ck, heads, scale, mode, f"attn_{tag}")
        return o, (q, k, v, cum, o32, lse)

    def bwd(saved, do):
        q, k, v, cum, o, lse = saved
        cq, ck = layouts(cum)
        res = _attn_bwd_call(q, k, v, cq, ck, o, lse, do, heads, scale, mode, f"attn_{tag}_bwd")
        dcum = res[3].reshape(heads, s) + res[4].reshape(heads, s) if mode == "fox" else jnp.zeros_like(cum)
        return res[0], res[1], res[2], dcum

    op.defvjp(fwd, bwd)
    return op(q, k, v, cum)


def _log_sigmoid(z):
    return jnp.minimum(z, 0.0) - jnp.log(1.0 + jnp.exp(-jnp.abs(z)))


def _tri_dot(x, tri):
    hi = x.astype(BF16)
    lo = (x - hi.astype(F32)).astype(BF16)
    return jnp.dot(hi, tri, preferred_element_type=F32) + jnp.dot(lo, tri, preferred_element_type=F32)


def _sb_call(q, k, v, keep_all, do, heads, scale, backward, name):
    s = q.shape[0]
    d = q.shape[1] // heads
    bq, bk = _sb_blocks(s)
    nq = s // bq

    def body(*refs):
        q_ref, k_ref, v_ref = refs[:3]
        qi = pl.program_id(1)
        qv = q_ref[...].astype(BF16)
        q_pos = qi * bq + lax.broadcasted_iota(jnp.int32, (bq, bk), 0)
        k_col = lax.broadcasted_iota(jnp.int32, (bq, bk), 1)
        r_i = lax.broadcasted_iota(jnp.int32, (bk, bk), 0)
        c_i = lax.broadcasted_iota(jnp.int32, (bk, bk), 1)
        hi = (qi * bq + bq - 1) // bk

        def logits(kj, diagonal):
            rows = pl.ds(pl.multiple_of(kj * bk, bk), bk)
            kb = k_ref[rows, :].astype(BF16)
            z = lax.dot_general(qv, kb, (((1,), (1,)), ((), ())), preferred_element_type=F32) * scale
            ls = _log_sigmoid(z)
            lk = ls - z
            past = None
            if diagonal:
                past = (kj * bk + k_col) < q_pos
                lk = jnp.where(past, lk, 0.0)
            return rows, kb, z, past, ls, lk

        if not backward:
            o_ref, keep_ref = refs[3:5]
            after = jnp.where(r_i > c_i, 1.0, 0.0).astype(BF16)

            def step(diagonal, it, carry):
                keep_right, acc = carry
                rows, _, _, past, ls, lk = logits(hi - it, diagonal)
                a = jnp.exp(ls + _tri_dot(lk, after) + keep_right)
                if diagonal:
                    a = jnp.where(past, a, 0.0)
                acc = acc + jnp.dot(a.astype(BF16), v_ref[rows, :].astype(BF16), preferred_element_type=F32)
                return keep_right + jnp.sum(lk, axis=1, keepdims=True), acc

            carry = (jnp.zeros((bq, 1), F32), jnp.zeros((bq, d), F32))
            carry = lax.fori_loop(0, 1, functools.partial(step, True), carry)
            keep, acc = lax.fori_loop(1, hi + 1, functools.partial(step, False), carry)
            o_ref[...] = acc.astype(o_ref.dtype)
            keep_ref[...] = keep
            return

        keep_ref, do_ref, dq_ref, dk_ref, dv_ref, dk_acc, dv_acc = refs[3:]

        @pl.when(qi == 0)
        def _():
            dk_acc[...] = jnp.zeros_like(dk_acc)
            dv_acc[...] = jnp.zeros_like(dv_acc)

        dov = do_ref[...].astype(BF16)
        keep_total = keep_ref[...]
        upto = jnp.where(r_i <= c_i, 1.0, 0.0).astype(BF16)
        before = jnp.where(r_i < c_i, 1.0, 0.0).astype(BF16)

        def step(diagonal, kj, carry):
            keep_left, g_left, dq = carry
            rows, kb, z, past, ls, lk = logits(kj, diagonal)
            between = keep_total - (keep_left + _tri_dot(lk, upto))
            a = jnp.exp(ls + between)
            if diagonal:
                a = jnp.where(past, a, 0.0)
            vb = v_ref[rows, :].astype(BF16)
            da = lax.dot_general(dov, vb, (((1,), (1,)), ((), ())), preferred_element_type=F32)
            g = a * da
            g_before = g_left + _tri_dot(g, before)
            dz = g * jnp.exp(ls - z) - jnp.exp(ls) * g_before
            if diagonal:
                dz = jnp.where(past, dz, 0.0)
            dzb = dz.astype(BF16)
            dk_acc[rows, :] += lax.dot_general(dzb, qv, (((0,), (0,)), ((), ())), preferred_element_type=F32)
            dv_acc[rows, :] += lax.dot_general(a.astype(BF16), dov, (((0,), (0,)), ((), ())),
                                               preferred_element_type=F32)
            dq = dq + jnp.dot(dzb, kb, preferred_element_type=F32)
            return (keep_left + jnp.sum(lk, axis=1, keepdims=True),
                    g_left + jnp.sum(g, axis=1, keepdims=True), dq)

        init = (jnp.zeros((bq, 1), F32), jnp.zeros((bq, 1), F32), jnp.zeros((bq, d), F32))
        carry = lax.fori_loop(0, hi, functools.partial(step, False), init)
        _, _, dq = lax.fori_loop(hi, hi + 1, functools.partial(step, True), carry)
        dq_ref[...] = (dq * scale).astype(dq_ref.dtype)

        @pl.when(qi == nq - 1)
        def _():
            dk_ref[...] = (dk_acc[...] * scale).astype(dk_ref.dtype)
            dv_ref[...] = dv_acc[...].astype(dv_ref.dtype)

    blk_q = pl.BlockSpec((bq, d), lambda h, i: (i, h))
    blk_kv = pl.BlockSpec((s, d), lambda h, i: (0, h))
    blk_row = pl.BlockSpec((None, bq, 1), lambda h, i: (h, i, 0))
    if not backward:
        return pl.pallas_call(
            body, name=name, grid=(heads, nq), in_specs=[blk_q, blk_kv, blk_kv], out_specs=[blk_q, blk_row],
            out_shape=[jax.ShapeDtypeStruct(q.shape, BF16), jax.ShapeDtypeStruct((heads, s, 1), F32)],
            compiler_params=_params(("parallel", "arbitrary")))(q, k, v)
    return pl.pallas_call(
        body, name=name, grid=(heads, nq), in_specs=[blk_q, blk_kv, blk_kv, blk_row, blk_q],
        out_specs=[blk_q, blk_kv, blk_kv],
        out_shape=[jax.ShapeDtypeStruct(q.shape, q.dtype), jax.ShapeDtypeStruct(k.shape, k.dtype),
                   jax.ShapeDtypeStruct(v.shape, v.dtype)],
        scratch_shapes=[pltpu.VMEM((s, d), F32), pltpu.VMEM((s, d), F32)],
        compiler_params=_params(("arbitrary", "arbitrary")))(q, k, v, keep_all, do)


def stick_breaking(q, k, v, heads, scale, tag):
    @jax.custom_vjp
    def op(q, k, v):
        return _sb_call(q, k, v, None, None, heads, scale, False, f"sb_{tag}")[0]

    def fwd(q, k, v):
        o, keep_all = _sb_call(q, k, v, None, None, heads, scale, False, f"sb_{tag}")
        return o, (q, k, v, keep_all)

    def bwd(saved, do):
        q, k, v, keep_all = saved
        return tuple(_sb_call(q, k, v, keep_all, do, heads, scale, True, f"sb_{tag}_bwd"))

    op.defvjp(fwd, bwd)
    return op(q, k, v)


def _tri_dot3(x, tri):
    hi = x.astype(BF16)
    r1 = x - hi.astype(F32)
    mid = r1.astype(BF16)
    lo = (r1 - mid.astype(F32)).astype(BF16)
    return (jnp.dot(hi, tri, preferred_element_type=F32) + jnp.dot(mid, tri, preferred_element_type=F32)
            + jnp.dot(lo, tri, preferred_element_type=F32))


def _gate_call(f, bias_b, dcum, name):
    heads, r, _ = f.shape
    backward = dcum is not None

    def body(*refs):
        f_ref, b_ref = refs[:2]
        r_i = lax.broadcasted_iota(jnp.int32, (LANES, LANES), 0)
        c_i = lax.broadcasted_iota(jnp.int32, (LANES, LANES), 1)
        x = f_ref[...] + b_ref[...]
        if not backward:
            o_ref, ls_ref = refs[2:]
            ls_ref[...] = _log_sigmoid(x)
            o_ref[...] = _tri_dot3(ls_ref[...], jnp.where(r_i <= c_i, 1.0, 0.0).astype(BF16))

            def row(i, carry):
                o_ref[pl.ds(i, 1), :] = o_ref[pl.ds(i, 1), :] + carry
                return carry + jnp.sum(ls_ref[pl.ds(i, 1), :], axis=1, keepdims=True)

            lax.fori_loop(0, r, row, jnp.zeros((1, 1), F32))
            return

        dc_ref, df_ref, db_ref, acc_ref = refs[2:]
        acc_ref[...] = _tri_dot3(dc_ref[...], jnp.where(r_i >= c_i, 1.0, 0.0).astype(BF16))

        def row(it, carry):
            i = r - 1 - it
            acc_ref[pl.ds(i, 1), :] = acc_ref[pl.ds(i, 1), :] + carry
            return carry + jnp.sum(dc_ref[pl.ds(i, 1), :], axis=1, keepdims=True)

        lax.fori_loop(0, r, row, jnp.zeros((1, 1), F32))
        df = acc_ref[...] * jnp.exp(_log_sigmoid(-x))
        df_ref[...] = df
        lane = lax.broadcasted_iota(jnp.int32, (1, LANES), 1)
        db_ref[...] = jnp.where(lane == 0, jnp.sum(df), 0.0)

    blk = pl.BlockSpec((None, r, LANES), lambda h: (h, 0, 0))
    blk_b = pl.BlockSpec((None, 1, LANES), lambda h: (h, 0, 0))
    if not backward:
        return pl.pallas_call(
            body, name=name, grid=(heads,), in_specs=[blk, blk_b], out_specs=blk,
            out_shape=jax.ShapeDtypeStruct(f.shape, F32), scratch_shapes=[pltpu.VMEM((r, LANES), F32)],
            compiler_params=_params(("parallel",)))(f, bias_b)
    return pl.pallas_call(
        body, name=name, grid=(heads,), in_specs=[blk, blk_b, blk], out_specs=[blk, blk_b],
        out_shape=[jax.ShapeDtypeStruct(f.shape, F32), jax.ShapeDtypeStruct(bias_b.shape, F32)],
        scratch_shapes=[pltpu.VMEM((r, LANES), F32)],
        compiler_params=_params(("parallel",)))(f, bias_b, dcum)


def forget_gate_cumsum(f, bias_b, tag):
    @jax.custom_vjp
    def op(f, bias_b):
        return _gate_call(f, bias_b, None, f"gate_{tag}")

    def fwd(f, bias_b):
        return _gate_call(f, bias_b, None, f"gate_{tag}"), (f, bias_b)

    def bwd(saved, dcum):
        return tuple(_gate_call(saved[0], saved[1], dcum, f"gate_{tag}_bwd"))

    op.defvjp(fwd, bwd)
    return op(f, bias_b)


def _swiglu_call(g, u, dact, name):
    s, n = g.shape
    ts = _tile(s, 256, 128)
    backward = dact is not None

    def body(*refs):
        gv = refs[0][...].astype(F32)
        uv = refs[1][...].astype(F32)
        sig = 1.0 / (1.0 + jnp.exp(-gv))
        if not backward:
            refs[2][...] = (gv * sig * uv).astype(refs[2].dtype)
            return
        dv = refs[2][...].astype(F32)
        refs[3][...] = (dv * uv * sig * (1.0 + gv * (1.0 - sig))).astype(refs[3].dtype)
        refs[4][...] = (dv * gv * sig).astype(refs[4].dtype)

    blk = pl.BlockSpec((ts, n), lambda i: (i, 0))
    shape = jax.ShapeDtypeStruct((s, n), g.dtype)
    if not backward:
        return pl.pallas_call(body, name=name, grid=(s // ts,), in_specs=[blk, blk], out_specs=blk,
                              out_shape=shape, compiler_params=_params(("parallel",)))(g, u)
    return pl.pallas_call(body, name=name, grid=(s // ts,), in_specs=[blk, blk, blk], out_specs=[blk, blk],
                          out_shape=[shape, shape], compiler_params=_params(("parallel",)))(g, u, dact)


def swiglu(g, u, tag):
    @jax.custom_vjp
    def op(g, u):
        return _swiglu_call(g, u, None, f"swiglu_{tag}")

    def fwd(g, u):
        return _swiglu_call(g, u, None, f"swiglu_{tag}"), (g, u)

    def bwd(saved, dact):
        return tuple(_swiglu_call(saved[0], saved[1], dact, f"swiglu_{tag}_bwd"))

    op.defvjp(fwd, bwd)
    return op(g, u)


def final_norm_loss(x, gain, target):
    s, d = x.shape
    ts = _tile(s, 256, 128)

    def body(x_ref, g_ref, t_ref, sq_ref, dx_ref, dg_ref):
        @pl.when(pl.program_id(0) == 0)
        def _():
            sq_ref[...] = jnp.zeros_like(sq_ref)
            dg_ref[...] = jnp.zeros_like(dg_ref)

        xv = x_ref[...]
        r = lax.rsqrt(jnp.mean(xv * xv, axis=1, keepdims=True) + EPS)
        xhat = xv * r
        err = xhat * g_ref[...] - t_ref[...]
        sq_ref[...] += jnp.sum(err * err)
        dy = err * (1.0 / d)
        gdy = dy * g_ref[...]
        dx_ref[...] = r * (gdy - xhat * jnp.mean(gdy * xhat, axis=1, keepdims=True))
        dg_ref[...] += jnp.sum(dy * xhat, axis=0, keepdims=True)

    blk = pl.BlockSpec((ts, d), lambda i: (i, 0))
    row = pl.BlockSpec((1, d), lambda i: (0, 0))
    return pl.pallas_call(
        body, name="final_norm_loss", grid=(s // ts,), in_specs=[blk, row, blk],
        out_specs=[pl.BlockSpec((1, LANES), lambda i: (0, 0)), blk, row],
        out_shape=[jax.ShapeDtypeStruct((1, LANES), F32), jax.ShapeDtypeStruct((s, d), F32),
                   jax.ShapeDtypeStruct((1, d), F32)],
        compiler_params=_params(("arbitrary",)))(x, gain.reshape(1, d), target)


def adamw(w, g, m, v, name):
    rows, cols = w.shape
    tr = _tile(rows, 256, 8)
    c1 = 1.0 - ADAM_B1 ** ADAM_STEP
    c2 = 1.0 - ADAM_B2 ** ADAM_STEP

    def body(w_ref, g_ref, m_ref, v_ref, d_ref, nm_ref, nv_ref):
        gv = g_ref[...]
        m_new = ADAM_B1 * m_ref[...] + (1.0 - ADAM_B1) * gv
        v_new = ADAM_B2 * v_ref[...] + (1.0 - ADAM_B2) * (gv * gv)
        d_ref[...] = -ADAM_LR * ((m_new / c1) / (jnp.sqrt(v_new / c2) + ADAM_EPS) + ADAM_WD * w_ref[...])
        nm_ref[...] = m_new
        nv_ref[...] = v_new

    blk = pl.BlockSpec((tr, cols), lambda i: (i, 0))
    shape = jax.ShapeDtypeStruct((rows, cols), F32)
    return pl.pallas_call(body, name=name, grid=(rows // tr,), in_specs=[blk] * 4, out_specs=[blk] * 3,
                          out_shape=[shape] * 3, compiler_params=_params(("parallel",)))(w, g, m, v)


ANY = pl.BlockSpec(memory_space=pl.ANY)


def _position():
    return lax.axis_index("x"), lax.axis_index("y"), lax.axis_index("c")


HBM = pl.BlockSpec(memory_space=pltpu.HBM)
SEM = pl.BlockSpec(memory_space=pltpu.SEMAPHORE)
DATAFLOW = pltpu.SideEffectType.DATAFLOW_SIDE_EFFECTING


def _gather_copies(srcs, lands, send_sems, recv_sems):
    x, y, c = _position()
    chips = [(1 - x, y), (x, 1 - y), (1 - x, 1 - y)]
    return [pltpu.make_async_remote_copy(
        src_ref=srcs[a], dst_ref=lands[a].at[2 * x + y], send_sem=send_sems.at[3 * a + k], recv_sem=recv_sems.at[3 * a + k],
        device_id=(chips[k][0], chips[k][1], c), device_id_type=MESH) for a in range(len(srcs)) for k in range(3)]


def _scatter_copies(srcs, lands, send_sems, recv_sems):
    x, y, c = _position()
    out = []
    for a in range(len(srcs)):
        for k in range(1, 8):
            px, py, pc = (1 - x if k & 4 else x, 1 - y if k & 2 else y, 1 - c if k & 1 else c)
            out.append(pltpu.make_async_remote_copy(
                src_ref=srcs[a].at[2 * px + py, pc], dst_ref=lands[a].at[4 * x + 2 * y + c],
                send_sem=send_sems.at[7 * a + k - 1], recv_sem=recv_sems.at[7 * a + k - 1],
                device_id=(px, py, pc), device_id_type=MESH))
    return out


def _swap_copies(srcs, lands, send_sems, recv_sems):
    x, y, c = _position()
    return [pltpu.make_async_remote_copy(
        src_ref=srcs[a], dst_ref=lands[a].at[c], send_sem=send_sems.at[a], recv_sem=recv_sems.at[a],
        device_id=(x, y, 1 - c), device_id_type=MESH) for a in range(len(srcs))]


EXCHANGES = {"gather": (_gather_copies, 3), "scatter": (_scatter_copies, 7), "swap": (_swap_copies, 1)}


def start_exchange(kind, srcs, lands, after, name):
    copies, per_array = EXCHANGES[kind]
    n = len(srcs)

    def body(*refs):
        for cp in copies(refs[:n], refs[n:2 * n], refs[2 * n + 1], refs[2 * n + 2]):
            cp.start()
        refs[-1][...] = jnp.zeros_like(refs[-1])

    sems = pltpu.SemaphoreType.DMA((n * per_array,))
    hbm = [pltpu.HBM(a.shape, a.dtype) for a in list(srcs) + list(lands)]
    outs = pl.pallas_call(
        body, name=name, in_specs=[HBM] * (2 * n) + [ANY],
        out_shape=(sems, sems, *hbm, jax.ShapeDtypeStruct((8, LANES), F32)),
        out_specs=(SEM, SEM, *[HBM] * (2 * n), pl.BlockSpec(memory_space=pltpu.VMEM)),
        input_output_aliases={i: 2 + i for i in range(2 * n)},
        compiler_params=pltpu.CompilerParams(has_side_effects=DATAFLOW),
    )(*[pltpu.with_memory_space_constraint(a, pltpu.HBM) for a in list(srcs) + list(lands)], after)
    return (kind, n, outs[:-1]), outs[-1]


def finish_exchange(state, after, name):
    kind, n, (send_sems, recv_sems, *buffers) = state
    copies, _ = EXCHANGES[kind]

    def body(*refs):
        for cp in copies(refs[:n], refs[n:2 * n], refs[2 * n], refs[2 * n + 1]):
            cp.wait_send()
            cp.wait_recv()

    outs = pl.pallas_call(
        body, name=name, in_specs=[HBM] * (2 * n) + [SEM, SEM, ANY],
        out_shape=[pltpu.HBM(a.shape, a.dtype) for a in buffers], out_specs=[HBM] * (2 * n),
        input_output_aliases={i: i for i in range(2 * n)},
        compiler_params=pltpu.CompilerParams(has_side_effects=DATAFLOW),
    )(*buffers, send_sems, recv_sems, after)
    return outs[n:]


def _own_slot(land_shape, dtype, block, slot):
    start = (slot,) + (0,) * block.ndim
    return lax.dynamic_update_slice(lax.empty(land_shape, dtype), block[None], start)


def broadcast_all(block, name):
    r, lanes = block.shape

    def body(b_ref, out_ref, send_sems, recv_sems, local_sem):
        x, y, c = _position()
        me = 4 * x + 2 * y + c
        mine = pltpu.make_async_copy(b_ref, out_ref.at[me], local_sem)
        mine.start()

        def copy(k, to_me):
            px, py, pc = (1 - x if k & 4 else x, 1 - y if k & 2 else y, 1 - c if k & 1 else c)
            slot = 4 * px + 2 * py + pc if to_me else me
            return pltpu.make_async_remote_copy(
                src_ref=b_ref, dst_ref=out_ref.at[slot], send_sem=send_sems.at[k - 1],
                recv_sem=recv_sems.at[k - 1], device_id=(px, py, pc), device_id_type=MESH)

        sends = [copy(k, False) for k in range(1, 8)]
        for cp in sends:
            cp.start()
        for k in range(1, 8):
            copy(k, True).wait_recv()
        for cp in sends:
            cp.wait_send()
        mine.wait()

    return pl.pallas_call(
        body, name=name, in_specs=[ANY], out_specs=ANY,
        out_shape=jax.ShapeDtypeStruct((8, r, lanes), block.dtype),
        scratch_shapes=[pltpu.SemaphoreType.DMA((7,)), pltpu.SemaphoreType.DMA((7,)), pltpu.SemaphoreType.DMA])(block)


def sum_slots(stack, name):
    n, r, lanes = stack.shape
    tr = _tile(r, max(16, (1 << 22) // (n * lanes * stack.dtype.itemsize) // 16 * 16), 16)

    def body(s_ref, o_ref):
        acc = s_ref[0].astype(F32)
        for i in range(1, n):
            acc = acc + s_ref[i].astype(F32)
        o_ref[...] = acc

    return pl.pallas_call(
        body, name=name, grid=(r // tr,), in_specs=[pl.BlockSpec((n, tr, lanes), lambda i: (0, i, 0))],
        out_specs=pl.BlockSpec((tr, lanes), lambda i: (i, 0)), out_shape=jax.ShapeDtypeStruct((r, lanes), F32),
        compiler_params=_params(("parallel",)))(stack)


def _pack(arrays, dtype, row_align):
    flat = jnp.concatenate([a.reshape(-1).astype(dtype) for a in arrays])
    rows = -(-flat.shape[0] // LANES)
    rows = -(-rows // row_align) * row_align
    return jnp.pad(flat, (0, rows * LANES - flat.shape[0])).reshape(rows, LANES)


def _unpack(flat, shapes):
    flat = flat.reshape(flat.shape[:-2] + (-1,))
    out, at = [], 0
    for shp in shapes:
        n = math.prod(shp)
        out.append(flat[..., at:at + n].reshape(flat.shape[:-1] + tuple(shp)))
        at += n
    return out


def _pad_cols(w):
    return jnp.pad(w, ((0, 0), (0, -w.shape[1] % LANES)))


def split_columns(raw, width, widths, dtypes):
    wp = raw.shape[1] * width // sum(widths)
    jn = raw.shape[1] // wp

    def cut(raw):
        whole = jnp.concatenate([raw[:, j * wp:j * wp + width] for j in range(jn)], axis=1)
        out, at = [], 0
        for n, dt in zip(widths, dtypes):
            out.append(whole[:, at:at + n].astype(dt))
            at += n
        return tuple(out)

    @jax.custom_vjp
    def op(raw):
        return cut(raw)

    def fwd(raw):
        return cut(raw), None

    def bwd(_, cts):
        whole = jnp.concatenate([ct.astype(BF16) for ct in cts], axis=1)
        gap = jnp.zeros((whole.shape[0], wp - width), BF16)
        blocks = []
        for j in range(jn):
            blocks += [whole[:, j * width:(j + 1) * width], gap]
        return (jnp.concatenate(blocks, axis=1).astype(raw.dtype),)

    op.defvjp(fwd, bwd)
    return op(raw)


def _mixer_half(x, mats, smalls, tables, widths, heads, tag):
    s, d_model = x.shape
    gw = heads * HEAD_DIM
    w_in, w_uq, w_ukv, w_out = mats
    attn_norm, q_norm, kv_norm, f_bias, group_norm = smalls
    rope_full, rope_mla = tables
    no_cum = jnp.zeros((heads, s), F32)

    h = rms_norm(x, attn_norm, 1, BF16, f"attn_{tag}")
    raw = matmul(h, w_in, F32, f"in_{tag}")
    sizes = [Q_LORA, KV_LORA, QK_ROPE, 2 * gw, gw, gw, gw, gw, heads, gw, gw, gw]
    kinds = [F32, F32, F32, F32, BF16, BF16, BF16, BF16, F32, BF16, BF16, BF16]
    q_lat, kv_lat, k_rope, qk_b, v_b, q_c, k_c, v_c, f_logit, q_d, k_d, v_d = split_columns(
        raw, widths[0], sizes, kinds)

    q_all = matmul(rms_norm(q_lat, q_norm, 1, BF16, f"qlat_{tag}"), w_uq, BF16, f"uq_{tag}")
    kv_all = matmul(rms_norm(kv_lat, kv_norm, 1, BF16, f"kvlat_{tag}"), w_ukv, BF16, f"ukv_{tag}")
    q_a = rope(q_all, rope_mla, True, BF16, f"qpe_{tag}")
    k_pe = rope(_pad_cols(k_rope), rope_mla, False, BF16, f"kpe_{tag}")
    kv4 = kv_all.reshape(s, heads, 2, LANES)
    k_a = jnp.stack([kv4[:, :, 0], jnp.broadcast_to(k_pe[:, None, :], (s, heads, LANES))], axis=2).reshape(s, 2 * gw)
    out_a = attention(q_a, k_a, kv4[:, :, 1].reshape(s, gw), no_cum, heads, (QK_NOPE + QK_ROPE) ** -0.5,
                      "causal", f"a_{tag}")

    qk_b = rope(qk_b, rope_full, False, BF16, f"qkb_{tag}")
    out_b = attention(qk_b[:, :gw], qk_b[:, gw:], v_b, no_cum, heads, HEAD_DIM ** -0.5, "dilated", f"b_{tag}")

    f_rows = f_logit.T.reshape(heads, s // LANES, LANES)
    bias_b = jnp.broadcast_to(f_bias[:, None, None], (heads, 1, LANES))
    cum = forget_gate_cumsum(f_rows, bias_b, tag).reshape(heads, s)
    out_c = attention(q_c, k_c, v_c, cum, heads, HEAD_DIM ** -0.5, "fox", f"c_{tag}")

    out_d = stick_breaking(q_d, k_d, v_d, heads, HEAD_DIM ** -0.5, f"d_{tag}")

    groups = jnp.concatenate([out_a, out_b, out_c, out_d], axis=1)
    return matmul_res(rms_norm(groups, group_norm, 4, BF16, f"group_{tag}"), w_out.reshape(-1, d_model), x, f"out_{tag}")


def _ffn_half(x, mats, smalls, tag):
    d_model = x.shape[1]
    w_gate, w_up, w_down = mats
    h2 = rms_norm(x, smalls[0], 1, BF16, f"ffn_{tag}")
    act = swiglu(matmul(h2, w_gate, BF16, f"gate_{tag}"), matmul(h2, w_up, BF16, f"up_{tag}"), tag)
    return matmul_res(act, w_down.reshape(-1, d_model), x, f"down_{tag}")


def kernel(x, attn_norm, w_in, mla_q_norm, w_uq, mla_kv_norm, w_ukv, fox_forget_bias, group_norm, w_out, ffn_norm, w_gate, w_up, w_down, final_norm, loss_target, m_attn_norm, m_w_in, m_mla_q_norm, m_w_uq, m_mla_kv_norm, m_w_ukv, m_fox_forget_bias, m_group_norm, m_w_out, m_ffn_norm, m_w_gate, m_w_up, m_w_down, m_final_norm, v_attn_norm, v_w_in, v_mla_q_norm, v_w_uq, v_mla_kv_norm, v_w_ukv, v_fox_forget_bias, v_group_norm, v_w_out, v_ffn_norm, v_w_gate, v_w_up, v_w_down, v_final_norm):
    depth = w_in.shape[0]
    _, s, d_model = x.shape
    heads = d_model // 4 // HEAD_DIM
    big = [w_in, w_uq, w_ukv, w_out, w_gate, w_up, w_down]
    big_m = [m_w_in, m_w_uq, m_w_ukv, m_w_out, m_w_gate, m_w_up, m_w_down]
    big_v = [v_w_in, v_w_uq, v_w_ukv, v_w_out, v_w_gate, v_w_up, v_w_down]
    small = [attn_norm, mla_q_norm, mla_kv_norm, fox_forget_bias, group_norm, ffn_norm]
    small_m = [m_attn_norm, m_mla_q_norm, m_mla_kv_norm, m_fox_forget_bias, m_group_norm, m_ffn_norm, m_final_norm]
    small_v = [v_attn_norm, v_mla_q_norm, v_mla_kv_norm, v_fox_forget_bias, v_group_norm, v_ffn_norm, v_final_norm]
    widths = [w.shape[2] for w in big]
    tables = (rope_tables(s, HEAD_DIM), rope_tables(s, QK_ROPE))

    px, py, pc = _position()

    n_units = 2 * depth
    mixer_w, mixer_s = 4, 5

    def unit_arrays(u, arrays, split):
        l, ffn = divmod(u, 2)
        return [a[l] for a in (arrays[split:] if ffn else arrays[:split])]

    def unit_fn(u):
        tag = str(u // 2)
        if u % 2:
            return lambda g, sm, xx: _ffn_half(xx, g, sm, tag)
        return lambda g, sm, xx: _mixer_half(xx, g, sm, tables, widths, heads, tag)

    def start_gather(u, after):
        shards = [_pad_cols(w).astype(BF16) for w in unit_arrays(u, big, mixer_w)]
        lands = [_own_slot((4,) + sh.shape, BF16, sh, 2 * px + py) for sh in shards]
        return start_exchange("gather", shards, lands, after, f"gather_start_{u}")

    state, token = start_gather(0, x)
    gathered = finish_exchange(state, token, "gather_wait_0")
    x_l, vjps = x[0], []
    for u in range(n_units):
        small_u = unit_arrays(u, small, mixer_s)
        if u + 1 < n_units:
            state, token = start_gather(u + 1, gathered[0])
            small_u[0] = small_u[0] + token[0, 0]
        x_l, vjp_u = jax.vjp(unit_fn(u), gathered, small_u, x_l)
        vjps.append(vjp_u)
        if u + 1 < n_units:
            gathered = finish_exchange(state, x_l, f"gather_wait_{u + 1}")

    sq, dx, d_final = final_norm_loss(x_l, final_norm, loss_target[0])
    loss = 0.5 / d_model * lax.psum(sq[0, 0], ("x", "y", "c"))

    d_small_units = [None] * n_units
    grad_units = [None] * n_units
    scatters, swaps = {}, {}

    def finish_scatter(u, after):
        recv = finish_exchange(scatters.pop(u), after, f"scatter_wait_{u}")
        halves = [sum_slots(r, f"sum_{u}_{i}") for i, r in enumerate(recv)]
        lands = [_own_slot((2,) + h.shape, F32, h, pc) for h in halves]
        swaps[u], token = start_exchange("swap", halves, lands, after, f"swap_start_{u}")
        return token

    def finish_swap(u, after):
        full = finish_exchange(swaps.pop(u), after, f"swap_wait_{u}")
        unit_widths = widths[mixer_w:] if u % 2 else widths[:mixer_w]
        grad_units[u] = [f.reshape(-1, f.shape[2])[:, :wd] for f, wd in zip(full, unit_widths)]

    for u in reversed(range(n_units)):
        d_g, d_small_units[u], dx = vjps[u](dx)
        parts = [g.reshape(4, 2, g.shape[1] // 2, g.shape[2]) for g in d_g]
        lands = [_own_slot((8,) + p.shape[2:], BF16, p[2 * px + py, pc], 4 * px + 2 * py + pc) for p in parts]
        scatters[u], token = start_exchange("scatter", parts, lands, dx, f"scatter_start_{u}")
        tokens = token[0, 0]
        if u + 2 < n_units:
            finish_swap(u + 2, dx)
        if u + 1 < n_units:
            tokens = tokens + finish_scatter(u + 1, dx)[0, 0]
        dx = dx + tokens
    finish_swap(1, dx)
    token = finish_scatter(0, dx)
    finish_swap(0, token)
    big_g = [jnp.stack([(grad_units[2 * l] + grad_units[2 * l + 1])[i] for l in range(depth)])
             for i in range(len(big))]
    d_small = [jnp.stack([(list(d_small_units[2 * l]) + list(d_small_units[2 * l + 1]))[i] for l in range(depth)])
               for i in range(len(small))]

    small_all = small + [final_norm]
    packed = _pack(d_small + [d_final], F32, 8)
    small_sum = sum_slots(broadcast_all(packed, "small_gather"), "small_sum")
    small_g = _unpack(small_sum, [p.shape for p in small_all])

    def update(w, g, m, v, name):
        shp = w.shape
        two_d = (-1, shp[-1])
        return [o.reshape(shp) for o in adamw(w.reshape(two_d), g.reshape(two_d), m.reshape(two_d),
                                              v.reshape(two_d), name)]

    big_u = [update(w, g, m, v, f"adamw_big{i}") for i, (w, g, m, v) in enumerate(zip(big, big_g, big_m, big_v))]
    small_u = adamw(_pack(small_all, F32, 8), small_sum, _pack(small_m, F32, 8), _pack(small_v, F32, 8), "adamw_small")
    small_u = [_unpack(u, [p.shape for p in small_all]) for u in small_u]

    def ordered(bigs, smalls):
        a_n, q_n, kv_n, f_b, g_n, f_n, fin = smalls
        wi, uq, ukv, wo, wg, wu, wd = bigs
        return [a_n, wi, q_n, uq, kv_n, ukv, f_b, g_n, wo, f_n, wg, wu, wd, fin]

    grads = ordered(big_g, small_g)
    deltas = ordered([u[0] for u in big_u], small_u[0])
    new_m = ordered([u[1] for u in big_u], small_u[1])
    new_v = ordered([u[2] for u in big_u], small_u[2])
    return (loss, dx[None], *grads, *deltas, *new_m, *new_v)
```
